```python
import jax
import jax.numpy as jnp
from jax import lax
import numpy as np

D_MODEL = 4096
BATCH = 32
SEQ = 256
DEPTH = 4
DEC_BATCH = 2
DEC_SEQ = 2048
PAST_LEN = 256

GRID_W = 64
N_HEADS = 16
QK_NOPE = 128
QK_ROPE = 64
V_DIM = 128
Q_LORA = 768
KV_LORA = 512
MLA_W = N_HEADS * V_DIM
ROPE_THETA = 10000.0
Q_BLOCK = 128
LRU_W = 2048
LRU_BLOCKS = 16
LRU_BLK = LRU_W // LRU_BLOCKS
CONV_W = 4
LRU_C = 8.0
MIX_W = MLA_W + LRU_W
IN_COLS = Q_LORA + KV_LORA + QK_ROPE + 2 * LRU_W
SPLITS = [Q_LORA, Q_LORA + KV_LORA, Q_LORA + KV_LORA + QK_ROPE, Q_LORA + KV_LORA + QK_ROPE + LRU_W]
D_FF = 11008
N_EXPERTS = 8
TOP_K = 2
D_EXPERT = 2048
N_DENSE = (DEPTH + 1) // 2
N_MOE = DEPTH // 2
EPS = 1e-6

kernel_name = 'hybrid_mla_rglru_diffusion_step'


def rmsnorm(x, g):
    xf = x.astype(jnp.float32)
    y = xf * lax.rsqrt(jnp.mean(xf * xf, axis=-1, keepdims=True) + EPS)
    return (y * g.astype(jnp.float32)).astype(x.dtype)


def modulation(cond, w, b):
    m = jnp.einsum('bd,de->be', jax.nn.silu(cond), w) + b
    return [t[:, None, :] for t in jnp.split(m, 6, axis=-1)]


def adaln(x, g, shift, scale):
    return rmsnorm(x, g) * (1.0 + scale) + shift


def axial_angles(n_tokens):
    rows = n_tokens // GRID_W
    r_idx, c_idx = jnp.meshgrid(jnp.arange(rows), jnp.arange(GRID_W), indexing='ij')
    half = QK_ROPE // 2
    inv = ROPE_THETA ** (-jnp.arange(half // 2, dtype=jnp.float32) * 2.0 / half)
    ang_r = r_idx.reshape(-1).astype(jnp.float32)[:, None] * inv[None, :]
    ang_c = c_idx.reshape(-1).astype(jnp.float32)[:, None] * inv[None, :]
    return ang_r, ang_c


def rope_half(x, ang):
    x1, x2 = jnp.split(x, 2, axis=-1)
    cos, sin = jnp.cos(ang).astype(x.dtype), jnp.sin(ang).astype(x.dtype)
    return jnp.concatenate([x1 * cos - x2 * sin, x2 * cos + x1 * sin], axis=-1)


def axial_rope(x, ang_r, ang_c):
    xr, xc = jnp.split(x, 2, axis=-1)
    return jnp.concatenate([rope_half(xr, ang_r), rope_half(xc, ang_c)], axis=-1)


def kv_up(ckv, w_kv_up):
    B, S, _ = ckv.shape
    kv = jnp.einsum('bsr,re->bse', ckv, w_kv_up).reshape(B, S, N_HEADS, QK_NOPE + V_DIM)
    return kv[..., :QK_NOPE], kv[..., QK_NOPE:]


def mla_attention(q_nope, q_rope, k_nope, k_rope, v):
    B, T, H, _ = q_nope.shape
    nb = T // Q_BLOCK
    scale = (QK_NOPE + QK_ROPE) ** -0.5

    def to_blocks(a):
        return jnp.moveaxis(a.reshape(B, nb, Q_BLOCK, *a.shape[2:]), 1, 0)

    def one_block(qb):
        qn, qr = qb
        s = jnp.einsum('bqhd,bkhd->bhqk', qn, k_nope) + jnp.einsum('bqhr,bkr->bhqk', qr, k_rope)
        p = jax.nn.softmax(s.astype(jnp.float32) * scale, axis=-1).astype(v.dtype)
        return jnp.einsum('bhqk,bkhd->bqhd', p, v)

    o = lax.map(one_block, (to_blocks(q_nope), to_blocks(q_rope)))
    return jnp.moveaxis(o, 0, 1).reshape(B, T, H * V_DIM)


def centred_dwconv(x, w, b):
    y = lax.conv_general_dilated(x, w.reshape(CONV_W, 1, LRU_W), window_strides=(1,),
                                 padding=[(2, 1)], dimension_numbers=('NWC', 'WIO', 'NWC'),
                                 feature_group_count=LRU_W)
    return y + b


def rglru_scan(x, w_a, b_a, w_x, b_x, lam, h0, reverse):
    B, T, W = x.shape
    xf = x.astype(jnp.float32)
    xb = xf.reshape(B, T, LRU_BLOCKS, LRU_BLK)
    r = jax.nn.sigmoid(jnp.einsum('btni,nij->btnj', xb, w_a.astype(jnp.float32)).reshape(B, T, W) + b_a)
    i = jax.nn.sigmoid(jnp.einsum('btni,nij->btnj', xb, w_x.astype(jnp.float32)).reshape(B, T, W) + b_x)
    log_a = -LRU_C * r * jax.nn.softplus(-lam.astype(jnp.float32))
    a = jnp.exp(log_a)
    u = jnp.sqrt(-jnp.expm1(2.0 * log_a)) * (i * xf)

    def combine(p, q):
        a1, b1 = p
        a2, b2 = q
        return a1 * a2, a2 * b1 + b2

    a_cum, b_cum = lax.associative_scan(combine, (a, u), reverse=reverse, axis=1)
    h = a_cum * h0.astype(jnp.float32)[:, None, :] + b_cum
    h_final = h[:, 0] if reverse else h[:, -1]
    return h, h_final


def token_mixer(h, lp, ctx_ckv=None, ctx_k_rope=None, h0=None, angles=None):
    B, T, _ = h.shape
    z = jnp.einsum('btd,de->bte', h, lp['w_in'])
    q_c, kv_c, k_rope, xb, gb = jnp.split(z, SPLITS, axis=-1)
    q = jnp.einsum('btr,re->bte', rmsnorm(q_c, lp['q_a_norm']), lp['w_q_up'])
    q = q.reshape(B, T, N_HEADS, QK_NOPE + QK_ROPE)
    q_nope, q_rope = q[..., :QK_NOPE], q[..., QK_NOPE:]
    ckv = rmsnorm(kv_c, lp['kv_a_norm'])
    k_nope, v = kv_up(ckv, lp['w_kv_up'])
    xconv = centred_dwconv(xb, lp['conv_w'], lp['conv_b'])
    if angles is None:
        attn = mla_attention(q_nope, q_rope, k_nope, k_rope, v)
        h0 = jnp.zeros((B, 2, LRU_W), jnp.float32)
    else:
        ang_r, ang_c = angles
        q_rope = axial_rope(q_rope, ang_r[:, None, :], ang_c[:, None, :])
        k_rope_lat = axial_rope(k_rope, ang_r, ang_c)
        c_nope, c_v = kv_up(ctx_ckv, lp['w_kv_up'])
        attn = mla_attention(q_nope, q_rope,
                             jnp.concatenate([c_nope, k_nope], axis=1),
                             jnp.concatenate([ctx_k_rope, k_rope_lat], axis=1),
                             jnp.concatenate([c_v, v], axis=1))
    hf, sf = rglru_scan(xconv, lp['lru_w_a'][0], lp['lru_b_a'][0], lp['lru_w_x'][0], lp['lru_b_x'][0],
                        lp['lru_lambda'][0], h0[:, 0], False)
    hb, sb = rglru_scan(xconv, lp['lru_w_a'][1], lp['lru_b_a'][1], lp['lru_w_x'][1], lp['lru_b_x'][1],
                        lp['lru_lambda'][1], h0[:, 1], True)
    rec = ((hf + hb) * jax.nn.gelu(gb.astype(jnp.float32))).astype(h.dtype)
    merged = jnp.concatenate([rmsnorm(attn, lp['out_norm_mla']), rmsnorm(rec, lp['out_norm_lru'])], axis=-1)
    out = jnp.einsum('btm,md->btd', merged, lp['w_out'])
    if angles is None:
        return out, (ckv, k_rope, jnp.stack([sf, sb], axis=1).astype(h.dtype))
    return out, None


def swiglu(x, wg, wu, wd):
    g = jnp.einsum('btd,df->btf', x, wg)
    u = jnp.einsum('btd,df->btf', x, wu)
    return jnp.einsum('btf,fd->btd', jax.nn.silu(g) * u, wd)


def moe_swiglu(x, router_w, wg, wu, wd):
    logits = jnp.einsum('btd,de->bte', x, router_w).astype(jnp.float32)
    top_v, top_i = lax.top_k(logits, TOP_K)
    gates = jax.nn.softmax(top_v, axis=-1)
    combine = jnp.sum(jax.nn.one_hot(top_i, N_EXPERTS, dtype=jnp.float32) * gates[..., None], axis=-2)
    y = jnp.zeros_like(x)
    for e in range(N_EXPERTS):
        y = y + combine[..., e:e + 1].astype(x.dtype) * swiglu(x, wg[e], wu[e], wd[e])
    return y


def setup_inputs(seed: int = 0) -> dict:
    key = jax.random.key(seed)
    ks = iter(jax.random.split(key, 48))
    f32 = jnp.float32

    def nrm(shape, scale):
        return jax.random.normal(next(ks), shape, f32) * scale

    def gain(shape):
        return 1.0 + nrm(shape, 0.01)

    a0 = jax.random.uniform(next(ks), (DEPTH, 2, LRU_W), f32, 0.9, 0.999)
    lam = jnp.log(a0) - jnp.log1p(-a0)
    return {
        'x_prompt': nrm((BATCH, SEQ, D_MODEL), 1.0),
        'x_sample': nrm((DEC_BATCH, DEC_SEQ, D_MODEL), 1.0),
        'c': nrm((DEC_BATCH, D_MODEL), 1.0),
        'cache_ckv': nrm((DEC_BATCH, DEPTH, PAST_LEN, KV_LORA), 1.0),
        'cache_k_rope': nrm((DEC_BATCH, DEPTH, PAST_LEN, QK_ROPE), 1.0),
        'state_lru': nrm((DEC_BATCH, DEPTH, 2, LRU_W), 0.5),
        'c_ctx': nrm((D_MODEL,), 1.0),
        'w_ada': nrm((DEPTH, D_MODEL, 6 * D_MODEL), D_MODEL ** -0.5),
        'b_ada': nrm((DEPTH, 6 * D_MODEL), 0.01),
        'norm_mix': gain((DEPTH, D_MODEL)),
        'norm_ffn': gain((DEPTH, D_MODEL)),
        'w_in': nrm((DEPTH, D_MODEL, IN_COLS), D_MODEL ** -0.5),
        'q_a_norm': gain((DEPTH, Q_LORA)),
        'w_q_up': nrm((DEPTH, Q_LORA, N_HEADS * (QK_NOPE + QK_ROPE)), Q_LORA ** -0.5),
        'kv_a_norm': gain((DEPTH, KV_LORA)),
        'w_kv_up': nrm((DEPTH, KV_LORA, N_HEADS * (QK_NOPE + V_DIM)), KV_LORA ** -0.5),
        'conv_w': nrm((DEPTH, CONV_W, LRU_W), CONV_W ** -0.5),
        'conv_b': nrm((DEPTH, LRU_W), 0.01),
        'lru_w_a': nrm((DEPTH, 2, LRU_BLOCKS, LRU_BLK, LRU_BLK), LRU_BLK ** -0.5),
        'lru_b_a': nrm((DEPTH, 2, LRU_W), 0.01),
        'lru_w_x': nrm((DEPTH, 2, LRU_BLOCKS, LRU_BLK, LRU_BLK), LRU_BLK ** -0.5),
        'lru_b_x': nrm((DEPTH, 2, LRU_W), 0.01),
        'lru_lambda': lam,
        'out_norm_mla': gain((DEPTH, MLA_W)),
        'out_norm_lru': gain((DEPTH, LRU_W)),
        'w_out': nrm((DEPTH, MIX_W, D_MODEL), MIX_W ** -0.5),
        'ffn_w_gate': nrm((N_DENSE, D_MODEL, D_FF), D_MODEL ** -0.5),
        'ffn_w_up': nrm((N_DENSE, D_MODEL, D_FF), D_MODEL ** -0.5),
        'ffn_w_down': nrm((N_DENSE, D_FF, D_MODEL), D_FF ** -0.5),
        'router_w': nrm((N_MOE, D_MODEL, N_EXPERTS), D_MODEL ** -0.5),
        'exp_w_gate': nrm((N_MOE, N_EXPERTS, D_MODEL, D_EXPERT), D_MODEL ** -0.5),
        'exp_w_up': nrm((N_MOE, N_EXPERTS, D_MODEL, D_EXPERT), D_MODEL ** -0.5),
        'exp_w_down': nrm((N_MOE, N_EXPERTS, D_EXPERT, D_MODEL), D_EXPERT ** -0.5),
        'final_norm': gain((D_MODEL,)),
    }


def reference(x_prompt, x_sample, c, cache_ckv, cache_k_rope, state_lru, c_ctx, w_ada, b_ada,
              norm_mix, norm_ffn, w_in, q_a_norm, w_q_up, kv_a_norm, w_kv_up, conv_w, conv_b,
              lru_w_a, lru_b_a, lru_w_x, lru_b_x, lru_lambda, out_norm_mla, out_norm_lru, w_out,
              ffn_w_gate, ffn_w_up, ffn_w_down, router_w, exp_w_gate, exp_w_up, exp_w_down,
              final_norm):
    angles = axial_angles(x_sample.shape[1])
    xp, xs = x_prompt, x_sample
    ckv_list, krope_list, lru_list = [], [], []
    for l in range(DEPTH):
        lp = {'w_in': w_in[l], 'q_a_norm': q_a_norm[l], 'w_q_up': w_q_up[l], 'kv_a_norm': kv_a_norm[l],
              'w_kv_up': w_kv_up[l], 'conv_w': conv_w[l], 'conv_b': conv_b[l],
              'lru_w_a': lru_w_a[l], 'lru_b_a': lru_b_a[l], 'lru_w_x': lru_w_x[l], 'lru_b_x': lru_b_x[l],
              'lru_lambda': lru_lambda[l], 'out_norm_mla': out_norm_mla[l],
              'out_norm_lru': out_norm_lru[l], 'w_out': w_out[l]}
        j = l // 2
        if l % 2 == 0:
            ffn = lambda t, j=j: swiglu(t, ffn_w_gate[j], ffn_w_up[j], ffn_w_down[j])
        else:
            ffn = lambda t, j=j: moe_swiglu(t, router_w[j], exp_w_gate[j], exp_w_up[j], exp_w_down[j])
        mc = modulation(c_ctx[None, :], w_ada[l], b_ada[l])
        ms = modulation(c, w_ada[l], b_ada[l])
        out, (ckv, krope, st) = token_mixer(adaln(xp, norm_mix[l], mc[0], mc[1]), lp)
        xp = xp + mc[2] * out
        xp = xp + mc[5] * ffn(adaln(xp, norm_ffn[l], mc[3], mc[4]))
        ckv_list.append(ckv)
        krope_list.append(krope)
        lru_list.append(st)
        out, _ = token_mixer(adaln(xs, norm_mix[l], ms[0], ms[1]), lp, cache_ckv[:, l], cache_k_rope[:, l],
                             state_lru[:, l], angles)
        xs = xs + ms[2] * out
        xs = xs + ms[5] * ffn(adaln(xs, norm_ffn[l], ms[3], ms[4]))
    y_prompt = rmsnorm(xp, final_norm)
    y_sample = rmsnorm(xs, final_norm)
    new_ckv = jnp.stack(ckv_list, axis=1)
    new_k_rope = jnp.stack(krope_list, axis=1)
    new_lru_state = jnp.stack(lru_list, axis=1)
    return (y_prompt, y_sample, new_ckv, new_k_rope, new_lru_state)
```

```python
import functools
from typing import NamedTuple

import jax
import jax.numpy as jnp
from jax import lax
from jax.experimental import pallas as pl
from jax.experimental.pallas import tpu as pltpu

F32 = jnp.float32
BF16 = jnp.bfloat16

LANES = 128
SUBLANES = 8
VMEM_LIMIT_BYTES = 52 * 1024 * 1024

QK_NOPE = 128
QK_ROPE = 64
V_DIM = 128
LRU_BLK = 128
HEAD_PITCH = 2 * LANES


class Cfg(NamedTuple):
    d_model: int = 4096
    batch: int = 32
    seq: int = 256
    depth: int = 4
    dec_batch: int = 2
    dec_seq: int = 2048
    past_len: int = 256
    grid_w: int = 64
    n_heads: int = 16
    q_lora: int = 768
    kv_lora: int = 512
    lru_w: int = 2048
    conv_w: int = 4
    lru_c: float = 8.0
    d_ff: int = 11008
    n_experts: int = 8
    d_expert: int = 2048
    eps: float = 1e-6
    rope_theta: float = 10000.0
    mm_tile: int = 1024
    row_tile: int = 256
    attn_tq: int = 256
    moe_tm: int = 512
    moe_tf: int = 256
    gather_rows: int = 256
    lru_cw: int = 256
    lru_chunk: int = 256

    @property
    def n_ctx(self):
        return self.batch * self.seq

    @property
    def n_lat(self):
        return self.dec_batch * self.dec_seq

    @property
    def n_all(self):
        return self.n_ctx + self.n_lat


def _cparams(sem):
    return pltpu.CompilerParams(dimension_semantics=sem, vmem_limit_bytes=VMEM_LIMIT_BYTES)


def _pick_tile(n, pref):
    t = pref
    while n % t:
        t //= 2
    assert t >= LANES or t == n, (n, pref)
    return t


def _round_up(n, m):
    return (n + m - 1) // m * m


def _mod_sel(cfg, rows_per_tile):
    def sel(i):
        row0 = i * rows_per_tile
        return jnp.where(row0 < cfg.n_ctx, 0, 1 + (row0 - cfg.n_ctx) // cfg.dec_seq)
    return sel


def _ada_kernel(c_ref, w_ref, b_ref, o_ref):
    c = c_ref[...]
    a = (c * jax.nn.sigmoid(c)).astype(BF16)
    o_ref[0] = jnp.dot(a, w_ref[0].astype(BF16), preferred_element_type=F32) + b_ref[0]


def ada_modulation(cond8, w_ada, b_ada):
    depth, d, n = w_ada.shape
    tn = _pick_tile(n, 512)
    return pl.pallas_call(
        _ada_kernel,
        grid=(depth, n // tn),
        in_specs=[
            pl.BlockSpec((SUBLANES, d), lambda l, j: (0, 0)),
            pl.BlockSpec((1, d, tn), lambda l, j: (l, 0, j)),
            pl.BlockSpec((1, 1, tn), lambda l, j: (l, 0, j)),
        ],
        out_specs=pl.BlockSpec((1, SUBLANES, tn), lambda l, j: (l, 0, j)),
        out_shape=jax.ShapeDtypeStruct((depth, SUBLANES, n), F32),
        compiler_params=_cparams(("arbitrary", "arbitrary")),
        name="ada_modulation",
    )(cond8, w_ada, b_ada.reshape(depth, 1, n))


def _rms(x, g, eps):
    return x * lax.rsqrt(jnp.mean(x * x, axis=-1, keepdims=True) + eps) * g


def _adaln_kernel(x_ref, g_ref, sh_ref, sc_ref, o_ref, *, eps):
    y = _rms(x_ref[...], g_ref[...], eps)
    o_ref[...] = (y * (1.0 + sc_ref[0]) + sh_ref[0]).astype(o_ref.dtype)


def adaln(cfg, x, g, shift, scale):
    n, d = x.shape
    tr = cfg.row_tile
    sel = _mod_sel(cfg, tr)
    return pl.pallas_call(
        functools.partial(_adaln_kernel, eps=cfg.eps),
        grid=(n // tr,),
        in_specs=[
            pl.BlockSpec((tr, d), lambda i: (i, 0)),
            pl.BlockSpec((1, d), lambda i: (0, 0)),
            pl.BlockSpec((1, 1, d), lambda i: (sel(i), 0, 0)),
            pl.BlockSpec((1, 1, d), lambda i: (sel(i), 0, 0)),
        ],
        out_specs=pl.BlockSpec((tr, d), lambda i: (i, 0)),
        out_shape=jax.ShapeDtypeStruct((n, d), BF16),
        compiler_params=_cparams(("arbitrary",)),
        name="adaln",
    )(x, g.reshape(1, d), shift, scale)


def _final_norm_kernel(x_ref, g_ref, o_ref, *, eps):
    o_ref[...] = _rms(x_ref[...], g_ref[...], eps)


def final_norm(cfg, x, g, row0, nrows):
    d = x.shape[1]
    tr = cfg.row_tile
    b0 = row0 // tr
    return pl.pallas_call(
        functools.partial(_final_norm_kernel, eps=cfg.eps),
        grid=(nrows // tr,),
        in_specs=[
            pl.BlockSpec((tr, d), lambda i: (b0 + i, 0)),
            pl.BlockSpec((1, d), lambda i: (0, 0)),
        ],
        out_specs=pl.BlockSpec((tr, d), lambda i: (i, 0)),
        out_shape=jax.ShapeDtypeStruct((nrows, d), F32),
        compiler_params=_cparams(("arbitrary",)),
        name="final_norm",
    )(x, g.reshape(1, d))


def _mm_kernel(*refs, nk, gated):
    if gated:
        x_ref, w_ref, r_ref, g_ref, o_ref, acc_ref = refs
    else:
        x_ref, w_ref, o_ref, acc_ref = refs
    k = pl.program_id(2)
    prod = jnp.dot(x_ref[...], w_ref[...], preferred_element_type=F32)

    @pl.when(k == 0)
    def _():
        acc_ref[...] = prod

    @pl.when(k > 0)
    def _():
        acc_ref[...] += prod

    @pl.when(k == nk - 1)
    def _():
        if gated:
            o_ref[...] = r_ref[...] + g_ref[0] * acc_ref[...]
        else:
            o_ref[...] = acc_ref[...].astype(o_ref.dtype)


def matmul(cfg, x, w, out_dtype, resid=None, gate=None):
    m, kd = x.shape
    n = w.shape[1]
    tm = _pick_tile(m, cfg.mm_tile)
    tn = _pick_tile(n, cfg.mm_tile)
    tk = _pick_tile(kd, cfg.mm_tile)
    nk = kd // tk
    gated = resid is not None
    in_specs = [
        pl.BlockSpec((tm, tk), lambda i, j, k: (i, k)),
        pl.BlockSpec((tk, tn), lambda i, j, k: (k, j)),
    ]
    args = [x, w]
    if gated:
        sel = _mod_sel(cfg, tm)
        in_specs += [
            pl.BlockSpec((tm, tn), lambda i, j, k: (i, j)),
            pl.BlockSpec((1, 1, tn), lambda i, j, k: (sel(i), 0, j)),
        ]
        args += [resid, gate]
    return pl.pallas_call(
        functools.partial(_mm_kernel, nk=nk, gated=gated),
        grid=(m // tm, n // tn, nk),
        in_specs=in_specs,
        out_specs=pl.BlockSpec((tm, tn), lambda i, j, k: (i, j)),
        out_shape=jax.ShapeDtypeStruct((m, n), out_dtype),
        scratch_shapes=[pltpu.VMEM((tm, tn), F32)],
        compiler_params=_cparams(("arbitrary", "arbitrary", "arbitrary")),
        name="matmul_gated" if gated else "matmul",
    )(*args)


def _swiglu_up_kernel(x_ref, wg_ref, wu_ref, o_ref, ag_ref, au_ref, *, nk):
    k = pl.program_id(2)
    x = x_ref[...]
    pg = jnp.dot(x, wg_ref[...], preferred_element_type=F32)
    pu = jnp.dot(x, wu_ref[...], preferred_element_type=F32)

    @pl.when(k == 0)
    def _():
        ag_ref[...] = pg
        au_ref[...] = pu

    @pl.when(k > 0)
    def _():
        ag_ref[...] += pg
        au_ref[...] += pu

    @pl.when(k == nk - 1)
    def _():
        g = ag_ref[...]
        o_ref[...] = (g * jax.nn.sigmoid(g) * au_ref[...]).astype(o_ref.dtype)


def swiglu_up(cfg, x, wg, wu):
    m, kd = x.shape
    n = wg.shape[1]
    tm = _pick_tile(m, cfg.mm_tile)
    tn = _pick_tile(n, cfg.mm_tile)
    tk = _pick_tile(kd, cfg.mm_tile)
    nk = kd // tk
    return pl.pallas_call(
        functools.partial(_swiglu_up_kernel, nk=nk),
        grid=(m // tm, n // tn, nk),
        in_specs=[
            pl.BlockSpec((tm, tk), lambda i, j, k: (i, k)),
            pl.BlockSpec((tk, tn), lambda i, j, k: (k, j)),
            pl.BlockSpec((tk, tn), lambda i, j, k: (k, j)),
        ],
        out_specs=pl.BlockSpec((tm, tn), lambda i, j, k: (i, j)),
        out_shape=jax.ShapeDtypeStruct((m, n), BF16),
        scratch_shapes=[pltpu.VMEM((tm, tn), F32), pltpu.VMEM((tm, tn), F32)],
        compiler_params=_cparams(("arbitrary", "arbitrary", "arbitrary")),
        name="swiglu_up",
    )(x, wg, wu)


def _q_up_kernel(x_ref, g_ref, w_ref, o_ref, *, eps):
    y = _rms(x_ref[...], g_ref[...], eps).astype(BF16)
    o_ref[...] = jnp.dot(y, w_ref[...], preferred_element_type=F32).astype(o_ref.dtype)


def q_up(cfg, z_mla, g, w, col_block):
    n = z_mla.shape[0]
    kd, nout = w.shape
    tm = _pick_tile(n, 512)
    tn = _pick_tile(nout, cfg.mm_tile)
    return pl.pallas_call(
        functools.partial(_q_up_kernel, eps=cfg.eps),
        grid=(n // tm, nout // tn),
        in_specs=[
            pl.BlockSpec((tm, kd), lambda i, j: (i, col_block)),
            pl.BlockSpec((1, kd), lambda i, j: (0, 0)),
            pl.BlockSpec((kd, tn), lambda i, j: (0, j)),
        ],
        out_specs=pl.BlockSpec((tm, tn), lambda i, j: (i, j)),
        out_shape=jax.ShapeDtypeStruct((n, nout), BF16),
        compiler_params=_cparams(("arbitrary", "arbitrary")),
        name="q_up",
    )(z_mla, g.reshape(1, kd), w)


def _kv_up_kernel(x_ref, g_ref, kr_ref, tab_ref, w_ref, kv_ref, ckv_ref, krot_ref, *, eps):
    j = pl.program_id(1)
    ckv = _rms(x_ref[...], g_ref[...], eps)
    kv_ref[...] = jnp.dot(ckv.astype(BF16), w_ref[...], preferred_element_type=F32).astype(kv_ref.dtype)

    @pl.when(j == 0)
    def _():
        ckv_ref[...] = ckv
        kr = kr_ref[...]
        tab = tab_ref[...]
        krot = kr[:, :LANES] * tab[:, :LANES] + kr[:, LANES:] * tab[:, LANES:]
        krot_ref[...] = krot.astype(krot_ref.dtype)


def kv_up(cfg, z_mla, g, k_tab, w):
    n = z_mla.shape[0]
    kd, nout = w.shape
    assert kd % (2 * LANES) == 0
    tm = _pick_tile(n, 512)
    tn = _pick_tile(nout, cfg.mm_tile)
    kr_block = kd // (2 * LANES)
    return pl.pallas_call(
        functools.partial(_kv_up_kernel, eps=cfg.eps),
        grid=(n // tm, nout // tn),
        in_specs=[
            pl.BlockSpec((tm, kd), lambda i, j: (i, 0)),
            pl.BlockSpec((1, kd), lambda i, j: (0, 0)),
            pl.BlockSpec((tm, 2 * LANES), lambda i, j: (i, kr_block)),
            pl.BlockSpec((tm, 2 * LANES), lambda i, j: (i, 0)),
            pl.BlockSpec((kd, tn), lambda i, j: (0, j)),
        ],
        out_specs=[
            pl.BlockSpec((tm, tn), lambda i, j: (i, j)),
            pl.BlockSpec((tm, kd), lambda i, j: (i, 0)),
            pl.BlockSpec((tm, LANES), lambda i, j: (i, 0)),
        ],
        out_shape=[
            jax.ShapeDtypeStruct((n, nout), BF16),
            jax.ShapeDtypeStruct((n, kd), F32),
            jax.ShapeDtypeStruct((n, LANES), BF16),
        ],
        compiler_params=_cparams(("arbitrary", "arbitrary")),
        name="kv_up",
    )(z_mla, g.reshape(1, kd), z_mla, k_tab, w)


def _cache_kv_kernel(x_ref, w_ref, o_ref):
    o_ref[...] = jnp.dot(x_ref[...].astype(BF16), w_ref[...], preferred_element_type=F32).astype(o_ref.dtype)


def cache_kv_up(cfg, ckv, w):
    n, kd = ckv.shape
    nout = w.shape[1]
    tm = _pick_tile(n, 512)
    tn = _pick_tile(nout, cfg.mm_tile)
    return pl.pallas_call(
        _cache_kv_kernel,
        grid=(n // tm, nout // tn),
        in_specs=[
            pl.BlockSpec((tm, kd), lambda i, j: (i, 0)),
            pl.BlockSpec((kd, tn), lambda i, j: (0, j)),
        ],
        out_specs=pl.BlockSpec((tm, tn), lambda i, j: (i, j)),
        out_shape=jax.ShapeDtypeStruct((n, nout), BF16),
        compiler_params=_cparams(("arbitrary", "arbitrary")),
        name="cache_kv_up",
    )(ckv, w)


def _attn_kernel(*refs, hpb, scale, has_cache):
    if has_cache:
        cs_ref, q_ref, kv_ref, kr_ref, kvc_ref, krc_ref, o_ref = refs
    else:
        cs_ref, q_ref, kv_ref, kr_ref, o_ref = refs
    cs = cs_ref[...]
    kr = kr_ref[...]
    nt = (((1,), (1,)), ((), ()))
    for h in range(hpb):
        c0 = h * HEAD_PITCH
        qn = q_ref[:, c0:c0 + LANES]
        qr = (q_ref[:, c0 + LANES:c0 + HEAD_PITCH].astype(F32) * cs).astype(BF16)
        qc = jnp.concatenate([qn, qr], axis=1)
        kc = jnp.concatenate([kv_ref[:, c0:c0 + LANES], kr], axis=1)
        v = kv_ref[:, c0 + LANES:c0 + HEAD_PITCH]
        s = lax.dot_general(qc, kc, nt, preferred_element_type=F32) * scale
        m = jnp.max(s, axis=-1, keepdims=True)
        if has_cache:
            kcc = jnp.concatenate([kvc_ref[:, c0:c0 + LANES], krc_ref[...]], axis=1)
            vc = kvc_ref[:, c0 + LANES:c0 + HEAD_PITCH]
            sc = lax.dot_general(qc, kcc, nt, preferred_element_type=F32) * scale
            m = jnp.maximum(m, jnp.max(sc, axis=-1, keepdims=True))
        p = jnp.exp(s - m)
        l = jnp.sum(p, axis=-1, keepdims=True)
        o = jnp.dot(p.astype(BF16), v, preferred_element_type=F32)
        if has_cache:
            pc = jnp.exp(sc - m)
            l = l + jnp.sum(pc, axis=-1, keepdims=True)
            o = o + jnp.dot(pc.astype(BF16), vc, preferred_element_type=F32)
        o_ref[:, h * V_DIM:(h + 1) * V_DIM] = o / l


def attention(cfg, q, kv, kr, q_tab, *, row0, n_seq, seq_len, tq, hpb, cache=None):
    nh = cfg.n_heads
    assert seq_len % tq == 0 and row0 % seq_len == 0 and nh % hpb == 0
    nq = seq_len // tq
    qb0 = row0 // tq
    sb0 = row0 // seq_len
    scale = float((QK_NOPE + QK_ROPE) ** -0.5)
    in_specs = [
        pl.BlockSpec((tq, LANES), lambda b, h, i: (qb0 + b * nq + i, 0)),
        pl.BlockSpec((tq, hpb * HEAD_PITCH), lambda b, h, i: (qb0 + b * nq + i, h)),
        pl.BlockSpec((seq_len, hpb * HEAD_PITCH), lambda b, h, i: (sb0 + b, h)),
        pl.BlockSpec((seq_len, LANES), lambda b, h, i: (sb0 + b, 0)),
    ]
    args = [q_tab, q, kv, kr]
    if cache is not None:
        kvc, krc = cache
        past = kvc.shape[0] // n_seq
        in_specs += [
            pl.BlockSpec((past, hpb * HEAD_PITCH), lambda b, h, i: (b, h)),
            pl.BlockSpec((past, LANES), lambda b, h, i: (b, 0)),
        ]
        args += [kvc, krc]
    return pl.pallas_call(
        functools.partial(_attn_kernel, hpb=hpb, scale=scale, has_cache=cache is not None),
        grid=(n_seq, nh // hpb, nq),
        in_specs=in_specs,
        out_specs=pl.BlockSpec((tq, hpb * V_DIM), lambda b, h, i: (b * nq + i, h)),
        out_shape=jax.ShapeDtypeStruct((n_seq * seq_len, nh * V_DIM), F32),
        compiler_params=_cparams(("arbitrary", "arbitrary", "arbitrary")),
        name="attention_cached" if cache is not None else "attention",
    )(*args)


def _softplus(x):
    return jnp.maximum(x, 0.0) + jnp.log1p(jnp.exp(-jnp.abs(x)))


def _lru_kernel(xb_ref, gb_ref, cw_ref, cb_ref, w_ref, b_ref, lam_ref, h0_ref,
                rec_ref, st_ref, af_ref, uf_ref, ab_ref, ub_ref, *, seq_len, chunk, nblk, lru_c):
    nchunk = seq_len // chunk
    cw = cw_ref[...]
    cb = cb_ref[...]
    sp = _softplus(-lam_ref[...])

    def gate_chunk(c, carry):
        r0 = pl.multiple_of(c * chunk, chunk)
        cur = xb_ref[pl.ds(r0, chunk), :]
        prev_start = pl.multiple_of(jnp.maximum(r0 - SUBLANES, 0), SUBLANES)
        next_start = pl.multiple_of(jnp.minimum(r0 + chunk, seq_len - SUBLANES), SUBLANES)
        prev = jnp.where(c > 0, xb_ref[pl.ds(prev_start, SUBLANES), :], 0.0)
        nxt = jnp.where(c < nchunk - 1, xb_ref[pl.ds(next_start, SUBLANES), :], 0.0)
        ext = jnp.concatenate([prev, cur, nxt], axis=0)
        rows = chunk + 2 * SUBLANES
        xm2 = pltpu.roll(ext, 2, axis=0)[SUBLANES:SUBLANES + chunk]
        xm1 = pltpu.roll(ext, 1, axis=0)[SUBLANES:SUBLANES + chunk]
        xp1 = pltpu.roll(ext, rows - 1, axis=0)[SUBLANES:SUBLANES + chunk]
        xc = cw[0:1] * xm2 + cw[1:2] * xm1 + cw[2:3] * cur + cw[3:4] * xp1 + cb
        for n in range(nblk):
            lo = n * LRU_BLK
            xn = xc[:, lo:lo + LRU_BLK]
            g = jnp.dot(xn.astype(BF16), w_ref[n], preferred_element_type=F32) + b_ref[n]
            for d, (a_ref, u_ref) in enumerate(((af_ref, uf_ref), (ab_ref, ub_ref))):
                r = jax.nn.sigmoid(g[:, (2 * d) * LRU_BLK:(2 * d + 1) * LRU_BLK])
                i = jax.nn.sigmoid(g[:, (2 * d + 1) * LRU_BLK:(2 * d + 2) * LRU_BLK])
                log_a = (-lru_c) * r * sp[d:d + 1, lo:lo + LRU_BLK]
                t = jnp.tanh(log_a)
                a_ref[pl.ds(r0, chunk), lo:lo + LRU_BLK] = jnp.exp(log_a)
                u_ref[pl.ds(r0, chunk), lo:lo + LRU_BLK] = jnp.sqrt(-2.0 * t / (1.0 - t)) * (i * xn)
        return carry

    lax.fori_loop(0, nchunk, gate_chunk, 0)

    def scan_step(t, carry):
        hf, hb = carry
        hf = af_ref[pl.ds(t, 1), :] * hf + uf_ref[pl.ds(t, 1), :]
        uf_ref[pl.ds(t, 1), :] = hf
        tb = seq_len - 1 - t
        hb = ab_ref[pl.ds(tb, 1), :] * hb + ub_ref[pl.ds(tb, 1), :]
        ub_ref[pl.ds(tb, 1), :] = hb
        return hf, hb

    h0 = h0_ref[0]
    hf, hb = lax.fori_loop(0, seq_len, scan_step, (h0[0:1], h0[1:2]), unroll=8)
    st_ref[0] = jnp.concatenate([hf, hb], axis=0)

    def out_chunk(c, carry):
        r0 = pl.multiple_of(c * chunk, chunk)
        h = uf_ref[pl.ds(r0, chunk), :] + ub_ref[pl.ds(r0, chunk), :]
        rec_ref[pl.ds(r0, chunk), :] = h * jax.nn.gelu(gb_ref[pl.ds(r0, chunk), :])
        return carry

    lax.fori_loop(0, nchunk, out_chunk, 0)


def rglru(cfg, z_lru, conv_w, conv_b, wcat, bcat, lam, h0, *, row0, n_seq, seq_len):
    w = cfg.lru_w
    cwid = min(cfg.lru_cw, w)
    nblk = cwid // LRU_BLK
    ncb = w // cwid
    chunk = min(cfg.lru_chunk, seq_len)
    sb0 = row0 // seq_len
    assert row0 % seq_len == 0 and seq_len % chunk == 0 and chunk % SUBLANES == 0
    return pl.pallas_call(
        functools.partial(_lru_kernel, seq_len=seq_len, chunk=chunk, nblk=nblk, lru_c=cfg.lru_c),
        grid=(n_seq, ncb),
        in_specs=[
            pl.BlockSpec((seq_len, cwid), lambda b, c: (sb0 + b, c)),
            pl.BlockSpec((seq_len, cwid), lambda b, c: (sb0 + b, ncb + c)),
            pl.BlockSpec((cfg.conv_w, cwid), lambda b, c: (0, c)),
            pl.BlockSpec((1, cwid), lambda b, c: (0, c)),
            pl.BlockSpec((nblk, LRU_BLK, 4 * LRU_BLK), lambda b, c: (c, 0, 0)),
            pl.BlockSpec((nblk, 1, 4 * LRU_BLK), lambda b, c: (c, 0, 0)),
            pl.BlockSpec((2, cwid), lambda b, c: (0, c)),
            pl.BlockSpec((1, 2, cwid), lambda b, c: (b, 0, c)),
        ],
        out_specs=[
            pl.BlockSpec((seq_len, cwid), lambda b, c: (b, c)),
            pl.BlockSpec((1, 2, cwid), lambda b, c: (b, 0, c)),
        ],
        out_shape=[
            jax.ShapeDtypeStruct((n_seq * seq_len, w), F32),
            jax.ShapeDtypeStruct((n_seq, 2, w), F32),
        ],
        scratch_shapes=[pltpu.VMEM((seq_len, cwid), F32) for _ in range(4)],
        compiler_params=_cparams(("arbitrary", "arbitrary")),
        name="rglru",
    )(z_lru, z_lru, conv_w, conv_b.reshape(1, w), wcat, bcat, lam, h0)


def _merge_kernel(ac_ref, al_ref, rc_ref, rl_ref, ga_ref, gr_ref, o_ref, *, n_ctx_tiles, eps, wa):
    i = pl.program_id(0)

    def emit(a_ref, r_ref):
        o_ref[:, :wa] = _rms(a_ref[...], ga_ref[...], eps).astype(o_ref.dtype)
        o_ref[:, wa:] = _rms(r_ref[...], gr_ref[...], eps).astype(o_ref.dtype)

    @pl.when(i < n_ctx_tiles)
    def _():
        emit(ac_ref, rc_ref)

    @pl.when(i >= n_ctx_tiles)
    def _():
        emit(al_ref, rl_ref)


def merge_norm(cfg, attn_c, attn_l, rec_c, rec_l, g_mla, g_lru):
    tr = cfg.row_tile
    wa = attn_c.shape[1]
    wr = rec_c.shape[1]
    nct = cfg.n_ctx // tr
    nlt = cfg.n_lat // tr
    ctx_map = lambda i: (jnp.minimum(i, nct - 1), 0)
    lat_map = lambda i: (jnp.maximum(i - nct, 0), 0)
    return pl.pallas_call(
        functools.partial(_merge_kernel, n_ctx_tiles=nct, eps=cfg.eps, wa=wa),
        grid=(nct + nlt,),
        in_specs=[
            pl.BlockSpec((tr, wa), ctx_map),
            pl.BlockSpec((tr, wa), lat_map),
            pl.BlockSpec((tr, wr), ctx_map),
            pl.BlockSpec((tr, wr), lat_map),
            pl.BlockSpec((1, wa), lambda i: (0, 0)),
            pl.BlockSpec((1, wr), lambda i: (0, 0)),
        ],
        out_specs=pl.BlockSpec((tr, wa + wr), lambda i: (i, 0)),
        out_shape=jax.ShapeDtypeStruct((cfg.n_all, wa + wr), BF16),
        compiler_params=_cparams(("arbitrary",)),
        name="merge_norm",
    )(attn_c, attn_l, rec_c, rec_l, g_mla.reshape(1, wa), g_lru.reshape(1, wr))


def _split_bf16(x):
    hi = x.astype(BF16)
    lo = (x - hi.astype(F32)).astype(BF16)
    return hi, lo


def _router_kernel(x_ref, g_ref, sh_ref, sc_ref, wh_ref, wl_ref, idx_ref, gate_ref, *, eps, n_experts):
    y = _rms(x_ref[...], g_ref[...], eps)
    y = y * (1.0 + sc_ref[0]) + sh_ref[0]
    yh, yl = _split_bf16(y)
    wh = wh_ref[...]
    logits = (jnp.dot(yh, wh, preferred_element_type=F32)
              + jnp.dot(yh, wl_ref[...], preferred_element_type=F32)
              + jnp.dot(yl, wh, preferred_element_type=F32))
    lane = lax.broadcasted_iota(jnp.int32, logits.shape, 1)
    neg = jnp.float32(-jnp.inf)
    l1 = jnp.where(lane < n_experts, logits, neg)
    m1 = jnp.max(l1, axis=-1, keepdims=True)
    i1 = jnp.min(jnp.where(l1 == m1, lane, LANES), axis=-1, keepdims=True)
    l2 = jnp.where(lane == i1, neg, l1)
    m2 = jnp.max(l2, axis=-1, keepdims=True)
    i2 = jnp.min(jnp.where(l2 == m2, lane, LANES), axis=-1, keepdims=True)
    e = jnp.exp(m2 - m1)
    den = 1.0 + e
    idx_ref[...] = jnp.where(lane == 0, i1, jnp.where(lane == 1, i2, 0))
    gate_ref[...] = jnp.where(lane == 0, 1.0 / den, jnp.where(lane == 1, e / den, 0.0))


def router_top2(cfg, x, g, shift, scale, wr_hi, wr_lo):
    n, d = x.shape
    tr = cfg.row_tile
    sel = _mod_sel(cfg, tr)
    return pl.pallas_call(
        functools.partial(_router_kernel, eps=cfg.eps, n_experts=cfg.n_experts),
        grid=(n // tr,),
        in_specs=[
            pl.BlockSpec((tr, d), lambda i: (i, 0)),
            pl.BlockSpec((1, d), lambda i: (0, 0)),
            pl.BlockSpec((1, 1, d), lambda i: (sel(i), 0, 0)),
            pl.BlockSpec((1, 1, d), lambda i: (sel(i), 0, 0)),
            pl.BlockSpec((d, LANES), lambda i: (0, 0)),
            pl.BlockSpec((d, LANES), lambda i: (0, 0)),
        ],
        out_specs=[
            pl.BlockSpec((tr, LANES), lambda i: (i, 0)),
            pl.BlockSpec((tr, LANES), lambda i: (i, 0)),
        ],
        out_shape=[
            jax.ShapeDtypeStruct((n, LANES), jnp.int32),
            jax.ShapeDtypeStruct((n, LANES), F32),
        ],
        compiler_params=_cparams(("arbitrary",)),
        name="router_top2",
    )(x, g.reshape(1, d), shift, scale, wr_hi, wr_lo)


def _gather_adaln_kernel(idx_ref, x_hbm, tok_ref, g_ref, sh_ref, sc_ref, o_ref, buf_ref, sem, *,
                         rows, eps, n_ctx, dec_seq, n_mod):
    def row_copy(r):
        return pltpu.make_async_copy(x_hbm.at[pl.ds(idx_ref[0, 0, r], 1), :], buf_ref.at[pl.ds(r, 1), :], sem)

    def start(r, c):
        row_copy(r).start()
        return c

    def wait(r, c):
        row_copy(r).wait()
        return c

    lax.fori_loop(0, rows, start, 0)
    lax.fori_loop(0, rows, wait, 0)
    y = _rms(buf_ref[...], g_ref[...], eps)
    tok = tok_ref[...]
    sh = sh_ref[0]
    sc = sc_ref[0]
    for b in range(1, n_mod):
        in_b = tok >= n_ctx + (b - 1) * dec_seq
        sh = jnp.where(in_b, sh_ref[b], sh)
        sc = jnp.where(in_b, sc_ref[b], sc)
    o_ref[...] = (y * (1.0 + sc) + sh).astype(o_ref.dtype)


def gather_adaln(cfg, x, idx_sorted, g, shift, scale):
    n, d = x.shape
    p = idx_sorted.shape[0]
    rows = cfg.gather_rows
    nsteps = p // rows
    n_mod = 1 + cfg.dec_batch
    return pl.pallas_call(
        functools.partial(_gather_adaln_kernel, rows=rows, eps=cfg.eps, n_ctx=cfg.n_ctx,
                          dec_seq=cfg.dec_seq, n_mod=n_mod),
        grid=(nsteps,),
        in_specs=[
            pl.BlockSpec((1, 1, rows), lambda i: (i, 0, 0), memory_space=pltpu.SMEM),
            pl.BlockSpec(memory_space=pl.ANY),
            pl.BlockSpec((rows, 1), lambda i: (i, 0)),
            pl.BlockSpec((1, d), lambda i: (0, 0)),
            pl.BlockSpec((SUBLANES, 1, d), lambda i: (0, 0, 0)),
            pl.BlockSpec((SUBLANES, 1, d), lambda i: (0, 0, 0)),
        ],
        out_specs=pl.BlockSpec((rows, d), lambda i: (i, 0)),
        out_shape=jax.ShapeDtypeStruct((p, d), BF16),
        scratch_shapes=[pltpu.VMEM((rows, d), F32), pltpu.SemaphoreType.DMA(())],
        compiler_params=_cparams(("arbitrary",)),
        name="moe_gather_adaln",
    )(idx_sorted.reshape(nsteps, 1, rows), x, idx_sorted.reshape(p, 1), g.reshape(1, d), shift, scale)


def _experts_kernel(te_ref, nu_ref, xs_ref, wg_ref, wu_ref, wd_ref, gs_ref, o_ref, *, nf):
    i = pl.program_id(0)
    f = pl.program_id(1)

    @pl.when(i < nu_ref[0])
    def _():
        x = xs_ref[...]
        g = jnp.dot(x, wg_ref[0], preferred_element_type=F32)
        u = jnp.dot(x, wu_ref[0], preferred_element_type=F32)
        h = (g * jax.nn.sigmoid(g) * u).astype(BF16)
        y = jnp.dot(h, wd_ref[0], preferred_element_type=F32)

        @pl.when(f == 0)
        def _():
            o_ref[...] = y

        @pl.when(f > 0)
        def _():
            o_ref[...] += y

        @pl.when(f == nf - 1)
        def _():
            o_ref[...] = gs_ref[...] * o_ref[...]

    @pl.when((i >= nu_ref[0]) & (f == 0))
    def _():
        o_ref[...] = jnp.zeros_like(o_ref)


def grouped_experts(cfg, xs, tile_expert, n_used, wg, wu, wd, gate_sorted):
    p, d = xs.shape
    fdim = wg.shape[2]
    tm = cfg.moe_tm
    tf = _pick_tile(fdim, cfg.moe_tf)
    nf = fdim // tf
    nt = p // tm

    def live(i, nu):
        return i < nu[0]

    def row_map(i, f, te, nu):
        return (jnp.minimum(i, nu[0] - 1), 0)

    def w_up_map(i, f, te, nu):
        return (te[i], 0, jnp.where(live(i, nu), f, nf - 1))

    def w_dn_map(i, f, te, nu):
        return (te[i], jnp.where(live(i, nu), f, nf - 1), 0)

    grid_spec = pltpu.PrefetchScalarGridSpec(
        num_scalar_prefetch=2,
        grid=(nt, nf),
        in_specs=[
            pl.BlockSpec((tm, d), row_map),
            pl.BlockSpec((1, d, tf), w_up_map),
            pl.BlockSpec((1, d, tf), w_up_map),
            pl.BlockSpec((1, tf, d), w_dn_map),
            pl.BlockSpec((tm, 1), row_map),
        ],
        out_specs=pl.BlockSpec((tm, d), lambda i, f, te, nu: (i, 0)),
    )
    return pl.pallas_call(
        functools.partial(_experts_kernel, nf=nf),
        grid_spec=grid_spec,
        out_shape=jax.ShapeDtypeStruct((p, d), F32),
        compiler_params=_cparams(("arbitrary", "arbitrary")),
        name="moe_experts",
    )(tile_expert, n_used, xs, wg, wu, wd, gate_sorted.reshape(p, 1))


def _combine_kernel(p1_ref, p2_ref, ys_hbm, x_ref, g_ref, o_ref, buf_ref, sem, *, rows):
    def row_copy(k, pos_ref, r):
        return pltpu.make_async_copy(ys_hbm.at[pl.ds(pos_ref[0, 0, r], 1), :],
                                     buf_ref.at[k, pl.ds(r, 1), :], sem)

    def start(r, c):
        row_copy(0, p1_ref, r).start()
        row_copy(1, p2_ref, r).start()
        return c

    def wait(r, c):
        row_copy(0, p1_ref, r).wait()
        row_copy(1, p2_ref, r).wait()
        return c

    lax.fori_loop(0, rows, start, 0)
    lax.fori_loop(0, rows, wait, 0)
    o_ref[...] = x_ref[...] + g_ref[0] * (buf_ref[0] + buf_ref[1])


def moe_combine(cfg, x, ys, pos1, pos2, gate):
    n, d = x.shape
    rows = cfg.gather_rows
    nsteps = n // rows
    sel = _mod_sel(cfg, rows)
    return pl.pallas_call(
        functools.partial(_combine_kernel, rows=rows),
        grid=(nsteps,),
        in_specs=[
            pl.BlockSpec((1, 1, rows), lambda i: (i, 0, 0), memory_space=pltpu.SMEM),
            pl.BlockSpec((1, 1, rows), lambda i: (i, 0, 0), memory_space=pltpu.SMEM),
            pl.BlockSpec(memory_space=pl.ANY),
            pl.BlockSpec((rows, d), lambda i: (i, 0)),
            pl.BlockSpec((1, 1, d), lambda i: (sel(i), 0, 0)),
        ],
        out_specs=pl.BlockSpec((rows, d), lambda i: (i, 0)),
        out_shape=jax.ShapeDtypeStruct((n, d), F32),
        scratch_shapes=[pltpu.VMEM((2, rows, d), F32), pltpu.SemaphoreType.DMA(())],
        compiler_params=_cparams(("arbitrary",)),
        name="moe_combine",
    )(pos1.reshape(nsteps, 1, rows), pos2.reshape(nsteps, 1, rows), ys, x, gate)


def _routing_tables(cfg, idx, gates):
    n = idx.shape[0]
    ne = cfg.n_experts
    tm = cfg.moe_tm
    p = _round_up(2 * n, tm) + ne * tm
    nt = p // tm
    flat_e = idx[:, :2].reshape(-1)
    flat_g = gates[:, :2].reshape(-1)
    onehot = (flat_e[:, None] == jnp.arange(ne, dtype=jnp.int32)[None, :]).astype(jnp.int32)
    csum = jnp.cumsum(onehot, axis=0)
    rank = jnp.sum(csum * onehot, axis=1) - 1
    counts = csum[-1]
    tiles_e = (counts + tm - 1) // tm
    tiles_end = jnp.cumsum(tiles_e)
    row_off = (tiles_end - tiles_e) * tm
    pos = jnp.sum(onehot * row_off[None, :], axis=1) + rank
    n_used = tiles_end[-1]
    tile_ids = jnp.arange(nt, dtype=jnp.int32)
    tile_expert = jnp.sum((tile_ids[:, None] >= tiles_end[None, :]).astype(jnp.int32), axis=1)
    last_expert = jnp.sum((n_used - 1 >= tiles_end).astype(jnp.int32))
    tile_expert = jnp.where(tile_ids < n_used, tile_expert, last_expert).astype(jnp.int32)
    tok = jnp.arange(2 * n, dtype=jnp.int32) // 2
    idx_sorted = jnp.zeros((p,), jnp.int32).at[pos].set(tok)
    gate_sorted = jnp.zeros((p,), F32).at[pos].set(flat_g)
    pos = pos.reshape(n, 2).astype(jnp.int32)
    return idx_sorted, gate_sorted, pos[:, 0], pos[:, 1], tile_expert, n_used.reshape(1).astype(jnp.int32)


def moe_ffn(cfg, x, g, shift, scale, gate, wr_hi, wr_lo, wg, wu, wd):
    idx, gates = router_top2(cfg, x, g, shift, scale, wr_hi, wr_lo)
    idx_sorted, gate_sorted, pos1, pos2, tile_expert, n_used = _routing_tables(cfg, idx, gates)
    xs = gather_adaln(cfg, x, idx_sorted, g, shift, scale)
    ys = grouped_experts(cfg, xs, tile_expert, n_used, wg, wu, wd, gate_sorted)
    return moe_combine(cfg, x, ys, pos1, pos2, gate)


def _rope_swap_perm():
    q = QK_ROPE // 4
    return jnp.array(list(range(q, 2 * q)) + list(range(0, q)) + list(range(3 * q, 4 * q)) + list(range(2 * q, 3 * q)))


def _rope_tables(cfg):
    t = jnp.arange(cfg.dec_seq)
    half = QK_ROPE // 2
    inv = cfg.rope_theta ** (-jnp.arange(half // 2, dtype=F32) * 2.0 / half)
    ang_r = (t // cfg.grid_w).astype(F32)[:, None] * inv[None, :]
    ang_c = (t % cfg.grid_w).astype(F32)[:, None] * inv[None, :]
    c = jnp.concatenate([jnp.cos(ang_r), jnp.cos(ang_r), jnp.cos(ang_c), jnp.cos(ang_c)], axis=1)
    s = jnp.concatenate([-jnp.sin(ang_r), jnp.sin(ang_r), -jnp.sin(ang_c), jnp.sin(ang_c)], axis=1)
    c = jnp.concatenate([jnp.ones((cfg.n_ctx, QK_ROPE), F32), jnp.tile(c, (cfg.dec_batch, 1))], axis=0)
    s = jnp.concatenate([jnp.zeros((cfg.n_ctx, QK_ROPE), F32), jnp.tile(s, (cfg.dec_batch, 1))], axis=0)
    return jnp.concatenate([c, s], axis=1), jnp.concatenate([c, c, s, s], axis=1)


def _prep_layer_weights(cfg, w_in, w_q_up, lru_w_a, lru_w_x, lru_b_a, lru_b_x):
    ql, kl, w = cfg.q_lora, cfg.kv_lora, cfg.lru_w
    perm = _rope_swap_perm()
    o_kv = ql
    o_kr = ql + kl
    o_xb = o_kr + QK_ROPE
    wq_c = w_in[:, :, :ql]
    wkv_c = w_in[:, :, o_kv:o_kr]
    wkr = w_in[:, :, o_kr:o_xb]
    wkr_sw = wkr[:, :, perm]
    w_lru = w_in[:, :, o_xb:].astype(BF16)
    w_mla = jnp.concatenate([wkv_c, wkr, wkr, wkr_sw, wkr_sw, wq_c], axis=2).astype(BF16)
    depth = w_q_up.shape[0]
    wq = w_q_up.reshape(depth, ql, cfg.n_heads, QK_NOPE + QK_ROPE)
    wq_rope = wq[..., QK_NOPE:]
    wq = jnp.concatenate([wq[..., :QK_NOPE], wq_rope, wq_rope[..., perm]], axis=-1)
    wq = wq.reshape(depth, ql, cfg.n_heads * HEAD_PITCH).astype(BF16)
    wcat = jnp.concatenate([lru_w_a[:, 0], lru_w_x[:, 0], lru_w_a[:, 1], lru_w_x[:, 1]], axis=-1).astype(BF16)
    nb = w // LRU_BLK
    bsplit = lambda b: b.reshape(depth, 2, nb, 1, LRU_BLK)
    ba, bx = bsplit(lru_b_a), bsplit(lru_b_x)
    bcat = jnp.concatenate([ba[:, 0], bx[:, 0], ba[:, 1], bx[:, 1]], axis=-1)
    return w_lru, w_mla, wq, wcat, bcat


def _pad_ff(w, axis, to):
    pad = [(0, 0)] * w.ndim
    pad[axis] = (0, to - w.shape[axis])
    return jnp.pad(w, pad)


def _forward(cfg, x_prompt, x_sample, c, cache_ckv, cache_k_rope, state_lru, c_ctx, w_ada, b_ada,
             norm_mix, norm_ffn, w_in, q_a_norm, w_q_up, kv_a_norm, w_kv_up, conv_w, conv_b,
             lru_w_a, lru_b_a, lru_w_x, lru_b_x, lru_lambda, out_norm_mla, out_norm_lru, w_out,
             ffn_w_gate, ffn_w_up, ffn_w_down, router_w, exp_w_gate, exp_w_up, exp_w_down,
             final_norm_g):
    d = cfg.d_model
    n_ctx, n_lat = cfg.n_ctx, cfg.n_lat
    n_mod = 1 + cfg.dec_batch
    assert n_mod <= SUBLANES

    x = jnp.concatenate([x_prompt.reshape(n_ctx, d), x_sample.reshape(n_lat, d)], axis=0)

    cond8 = jnp.zeros((SUBLANES, d), F32).at[0].set(c_ctx).at[1:n_mod].set(c)
    mods = ada_modulation(cond8, w_ada, b_ada)
    mods = mods.reshape(cfg.depth, SUBLANES, 6, 1, d)

    w_lru, w_mla, wq, wcat, bcat = _prep_layer_weights(cfg, w_in, w_q_up, lru_w_a, lru_w_x, lru_b_a, lru_b_x)
    w_kv = w_kv_up.astype(BF16)
    w_o = w_out.astype(BF16)
    ff_pad = _round_up(cfg.d_ff, cfg.mm_tile)
    ffg = _pad_ff(ffn_w_gate, 2, ff_pad).astype(BF16)
    ffu = _pad_ff(ffn_w_up, 2, ff_pad).astype(BF16)
    ffd = _pad_ff(ffn_w_down, 1, ff_pad).astype(BF16)
    exg = exp_w_gate.astype(BF16)
    exu = exp_w_up.astype(BF16)
    exd = exp_w_down.astype(BF16)
    wr = _pad_ff(router_w, 2, LANES)
    wr_hi = wr.astype(BF16)
    wr_lo = (wr - wr_hi.astype(F32)).astype(BF16)

    q_tab, k_tab = _rope_tables(cfg)
    h0_ctx = jnp.zeros((cfg.batch, 2, cfg.lru_w), F32)
    krc_all = jnp.concatenate([cache_k_rope, cache_k_rope], axis=-1).astype(BF16)

    ckv_list, krope_list, lru_list = [], [], []
    for l in range(cfg.depth):
        shift_m, scale_m, gate_m, shift_f, scale_f, gate_f = (mods[l, :, k] for k in range(6))

        h1 = adaln(cfg, x, norm_mix[l], shift_m, scale_m)
        z_lru = matmul(cfg, h1, w_lru[l], F32)
        z_mla = matmul(cfg, h1, w_mla[l], F32)
        q = q_up(cfg, z_mla, q_a_norm[l], wq[l], col_block=(cfg.kv_lora + 2 * LANES) // cfg.q_lora)
        kv, ckv, krot = kv_up(cfg, z_mla, kv_a_norm[l], k_tab, w_kv[l])
        kvc = cache_kv_up(cfg, cache_ckv[:, l].reshape(cfg.dec_batch * cfg.past_len, cfg.kv_lora), w_kv[l])
        krc = krc_all[:, l].reshape(cfg.dec_batch * cfg.past_len, LANES)

        attn_c = attention(cfg, q, kv, krot, q_tab, row0=0, n_seq=cfg.batch, seq_len=cfg.seq,
                           tq=cfg.seq, hpb=cfg.n_heads)
        attn_l = attention(cfg, q, kv, krot, q_tab, row0=n_ctx, n_seq=cfg.dec_batch, seq_len=cfg.dec_seq,
                           tq=min(cfg.attn_tq, cfg.dec_seq), hpb=1, cache=(kvc, krc))
        lam = lru_lambda[l]
        rec_c, st_c = rglru(cfg, z_lru, conv_w[l], conv_b[l], wcat[l], bcat[l], lam, h0_ctx,
                            row0=0, n_seq=cfg.batch, seq_len=cfg.seq)
        rec_l, _ = rglru(cfg, z_lru, conv_w[l], conv_b[l], wcat[l], bcat[l], lam, state_lru[:, l],
                         row0=n_ctx, n_seq=cfg.dec_batch, seq_len=cfg.dec_seq)
        merged = merge_norm(cfg, attn_c, attn_l, rec_c, rec_l, out_norm_mla[l], out_norm_lru[l])
        x = matmul(cfg, merged, w_o[l], F32, resid=x, gate=gate_m)

        ckv_list.append(ckv[:n_ctx].reshape(cfg.batch, cfg.seq, cfg.kv_lora))
        krope_list.append(z_mla[:n_ctx, cfg.kv_lora:cfg.kv_lora + QK_ROPE].reshape(cfg.batch, cfg.seq, QK_ROPE))
        lru_list.append(st_c)

        j = l // 2
        if l % 2 == 0:
            h2 = adaln(cfg, x, norm_ffn[l], shift_f, scale_f)
            hh = swiglu_up(cfg, h2, ffg[j], ffu[j])
            x = matmul(cfg, hh, ffd[j], F32, resid=x, gate=gate_f)
        else:
            x = moe_ffn(cfg, x, norm_ffn[l], shift_f, scale_f, gate_f, wr_hi[j], wr_lo[j],
                        exg[j], exu[j], exd[j])

    y_prompt = final_norm(cfg, x, final_norm_g, 0, n_ctx).reshape(cfg.batch, cfg.seq, d)
    y_sample = final_norm(cfg, x, final_norm_g, n_ctx, n_lat).reshape(cfg.dec_batch, cfg.dec_seq, d)
    new_ckv = jnp.stack(ckv_list, axis=1)
    new_k_rope = jnp.stack(krope_list, axis=1)
    new_lru_state = jnp.stack(lru_list, axis=1)
    return (y_prompt, y_sample, new_ckv, new_k_rope, new_lru_state)


def kernel(x_prompt, x_sample, c, cache_ckv, cache_k_rope, state_lru, c_ctx, w_ada, b_ada, norm_mix, norm_ffn, w_in, q_a_norm, w_q_up, kv_a_norm, w_kv_up, conv_w, conv_b, lru_w_a, lru_b_a, lru_w_x, lru_b_x, lru_lambda, out_norm_mla, out_norm_lru, w_out, ffn_w_gate, ffn_w_up, ffn_w_down, router_w, exp_w_gate, exp_w_up, exp_w_down, final_norm):
    return _forward(Cfg(), x_prompt, x_sample, c, cache_ckv, cache_k_rope, state_lru, c_ctx, w_ada, b_ada,
                    norm_mix, norm_ffn, w_in, q_a_norm, w_q_up, kv_a_norm, w_kv_up, conv_w, conv_b,
                    lru_w_a, lru_b_a, lru_w_x, lru_b_x, lru_lambda, out_norm_mla, out_norm_lru, w_out,
                    ffn_w_gate, ffn_w_up, ffn_w_down, router_w, exp_w_gate, exp_w_up, exp_w_down,
                    final_norm)
```

```python
import functools
from typing import NamedTuple

import jax
import jax.numpy as jnp
from jax import lax
from jax.experimental import pallas as pl
from jax.experimental.pallas import tpu as pltpu

F32 = jnp.float32
BF16 = jnp.bfloat16

LANES = 128
SUBLANES = 8
VMEM_LIMIT_BYTES = 56 * 1024 * 1024

QK_NOPE = 128
QK_ROPE = 64
V_DIM = 128
LRU_BLK = 128
HEAD_PITCH = 2 * LANES


class Cfg(NamedTuple):
    d_model: int = 4096
    batch: int = 32
    seq: int = 256
    depth: int = 4
    dec_batch: int = 2
    dec_seq: int = 2048
    past_len: int = 256
    grid_w: int = 64
    n_heads: int = 16
    q_lora: int = 768
    kv_lora: int = 512
    lru_w: int = 2048
    conv_w: int = 4
    lru_c: float = 8.0
    d_ff: int = 11008
    n_experts: int = 8
    d_expert: int = 2048
    eps: float = 1e-6
    rope_theta: float = 10000.0
    mm_tm: int = 1024
    mm_tn: int = 512
    down_tm: int = 512
    ff_align: int = 1024
    row_tile: int = 256
    attn_tq: int = 256
    attn_hpb: int = 2
    moe_tm: int = 512
    moe_tf: int = 512
    moe_tn: int = 512
    gather_rows: int = 512
    combine_rows: int = 256
    lru_cw: int = 256
    lru_chunk: int = 256

    @property
    def n_ctx(self):
        return self.batch * self.seq

    @property
    def n_lat(self):
        return self.dec_batch * self.dec_seq

    @property
    def n_all(self):
        return self.n_ctx + self.n_lat


def _cparams(sem):
    return pltpu.CompilerParams(dimension_semantics=sem, vmem_limit_bytes=VMEM_LIMIT_BYTES)


def _pick_tile(n, pref):
    t = pref
    while n % t:
        t //= 2
    assert t >= LANES or t == n, (n, pref)
    return t


def _round_up(n, m):
    return (n + m - 1) // m * m


def _mod_sel(cfg, rows_per_tile):
    def sel(i):
        row0 = i * rows_per_tile
        return jnp.where(row0 < cfg.n_ctx, 0, 1 + (row0 - cfg.n_ctx) // cfg.dec_seq)
    return sel


def _ada_kernel(c_ref, w_ref, b_ref, o_ref):
    c = c_ref[...]
    a = (c * jax.nn.sigmoid(c)).astype(BF16)
    o_ref[0] = jnp.dot(a, w_ref[0].astype(BF16), preferred_element_type=F32) + b_ref[0]


def ada_modulation(cond8, w_ada, b_ada):
    depth, d, n = w_ada.shape
    tn = _pick_tile(n, 512)
    return pl.pallas_call(
        _ada_kernel,
        grid=(depth, n // tn),
        in_specs=[
            pl.BlockSpec((SUBLANES, d), lambda l, j: (0, 0)),
            pl.BlockSpec((1, d, tn), lambda l, j: (l, 0, j)),
            pl.BlockSpec((1, 1, tn), lambda l, j: (l, 0, j)),
        ],
        out_specs=pl.BlockSpec((1, SUBLANES, tn), lambda l, j: (l, 0, j)),
        out_shape=jax.ShapeDtypeStruct((depth, SUBLANES, n), F32),
        compiler_params=_cparams(("arbitrary", "arbitrary")),
        name="ada_modulation",
    )(cond8, w_ada, b_ada.reshape(depth, 1, n))


def _rms(x, g, eps):
    return x * lax.rsqrt(jnp.mean(x * x, axis=-1, keepdims=True) + eps) * g


def _adaln_kernel(x_ref, g_ref, sh_ref, sc_ref, o_ref, *, eps):
    y = _rms(x_ref[...], g_ref[...], eps)
    o_ref[...] = (y * (1.0 + sc_ref[0]) + sh_ref[0]).astype(o_ref.dtype)


def adaln(cfg, x, g, shift, scale):
    n, d = x.shape
    tr = cfg.row_tile
    sel = _mod_sel(cfg, tr)
    return pl.pallas_call(
        functools.partial(_adaln_kernel, eps=cfg.eps),
        grid=(n // tr,),
        in_specs=[
            pl.BlockSpec((tr, d), lambda i: (i, 0)),
            pl.BlockSpec((1, d), lambda i: (0, 0)),
            pl.BlockSpec((1, 1, d), lambda i: (sel(i), 0, 0)),
            pl.BlockSpec((1, 1, d), lambda i: (sel(i), 0, 0)),
        ],
        out_specs=pl.BlockSpec((tr, d), lambda i: (i, 0)),
        out_shape=jax.ShapeDtypeStruct((n, d), BF16),
        compiler_params=_cparams(("arbitrary",)),
        name="adaln",
    )(x, g.reshape(1, d), shift, scale)


def _final_norm_kernel(x_ref, g_ref, o_ref, *, eps):
    o_ref[...] = _rms(x_ref[...], g_ref[...], eps)


def final_norm(cfg, x, g, row0, nrows):
    d = x.shape[1]
    tr = cfg.row_tile
    b0 = row0 // tr
    return pl.pallas_call(
        functools.partial(_final_norm_kernel, eps=cfg.eps),
        grid=(nrows // tr,),
        in_specs=[
            pl.BlockSpec((tr, d), lambda i: (b0 + i, 0)),
            pl.BlockSpec((1, d), lambda i: (0, 0)),
        ],
        out_specs=pl.BlockSpec((tr, d), lambda i: (i, 0)),
        out_shape=jax.ShapeDtypeStruct((nrows, d), F32),
        compiler_params=_cparams(("arbitrary",)),
        name="final_norm",
    )(x, g.reshape(1, d))


def _mm_kernel(*refs, gated):
    if gated:
        x_ref, w_ref, r_ref, g_ref, o_ref = refs
    else:
        x_ref, w_ref, o_ref = refs
    acc = jnp.dot(x_ref[...], w_ref[...], preferred_element_type=F32)
    if gated:
        o_ref[...] = r_ref[...] + g_ref[0] * acc
    else:
        o_ref[...] = acc.astype(o_ref.dtype)


def matmul(cfg, x, w, out_dtype, *, tm, tn, resid=None, gate=None):
    m, kd = x.shape
    n = w.shape[1]
    tm = _pick_tile(m, tm)
    tn = _pick_tile(n, tn)
    gated = resid is not None
    in_specs = [
        pl.BlockSpec((tm, kd), lambda i, j: (i, 0)),
        pl.BlockSpec((kd, tn), lambda i, j: (0, j)),
    ]
    args = [x, w]
    if gated:
        sel = _mod_sel(cfg, tm)
        in_specs += [
            pl.BlockSpec((tm, tn), lambda i, j: (i, j)),
            pl.BlockSpec((1, 1, tn), lambda i, j: (sel(i), 0, j)),
        ]
        args += [resid, gate]
    return pl.pallas_call(
        functools.partial(_mm_kernel, gated=gated),
        grid=(m // tm, n // tn),
        in_specs=in_specs,
        out_specs=pl.BlockSpec((tm, tn), lambda i, j: (i, j)),
        out_shape=jax.ShapeDtypeStruct((m, n), out_dtype),
        compiler_params=_cparams(("arbitrary", "arbitrary")),
        name="matmul_gated" if gated else "matmul",
    )(*args)


def _swiglu_up_kernel(x_ref, wg_ref, wu_ref, o_ref):
    x = x_ref[...]
    g = jnp.dot(x, wg_ref[...], preferred_element_type=F32)
    u = jnp.dot(x, wu_ref[...], preferred_element_type=F32)
    o_ref[...] = (g * jax.nn.sigmoid(g) * u).astype(o_ref.dtype)


def swiglu_up(cfg, x, wg, wu):
    m, kd = x.shape
    n = wg.shape[1]
    tm = _pick_tile(m, cfg.mm_tm)
    tn = _pick_tile(n, cfg.mm_tn)
    return pl.pallas_call(
        _swiglu_up_kernel,
        grid=(m // tm, n // tn),
        in_specs=[
            pl.BlockSpec((tm, kd), lambda i, j: (i, 0)),
            pl.BlockSpec((kd, tn), lambda i, j: (0, j)),
            pl.BlockSpec((kd, tn), lambda i, j: (0, j)),
        ],
        out_specs=pl.BlockSpec((tm, tn), lambda i, j: (i, j)),
        out_shape=jax.ShapeDtypeStruct((m, n), BF16),
        compiler_params=_cparams(("arbitrary", "arbitrary")),
        name="swiglu_up",
    )(x, wg, wu)


def _q_up_kernel(x_ref, g_ref, w_ref, o_ref, *, eps):
    y = _rms(x_ref[...], g_ref[...], eps).astype(BF16)
    o_ref[...] = jnp.dot(y, w_ref[...], preferred_element_type=F32).astype(o_ref.dtype)


def q_up(cfg, z, g, w, col_block):
    n = z.shape[0]
    kd, nout = w.shape
    tm = _pick_tile(n, 512)
    tn = _pick_tile(nout, 1024)
    return pl.pallas_call(
        functools.partial(_q_up_kernel, eps=cfg.eps),
        grid=(n // tm, nout // tn),
        in_specs=[
            pl.BlockSpec((tm, kd), lambda i, j: (i, col_block)),
            pl.BlockSpec((1, kd), lambda i, j: (0, 0)),
            pl.BlockSpec((kd, tn), lambda i, j: (0, j)),
        ],
        out_specs=pl.BlockSpec((tm, tn), lambda i, j: (i, j)),
        out_shape=jax.ShapeDtypeStruct((n, nout), BF16),
        compiler_params=_cparams(("arbitrary", "arbitrary")),
        name="q_up",
    )(z, g.reshape(1, kd), w)


def _kv_up_kernel(x_ref, g_ref, kr_ref, tab_ref, w_ref, kv_ref, ckv_ref, krot_ref, *, eps):
    j = pl.program_id(1)
    ckv = _rms(x_ref[...], g_ref[...], eps)
    kv_ref[...] = jnp.dot(ckv.astype(BF16), w_ref[...], preferred_element_type=F32).astype(kv_ref.dtype)

    @pl.when(j == 0)
    def _():
        ckv_ref[...] = ckv
        kr = kr_ref[...]
        tab = tab_ref[...]
        krot = kr[:, :LANES] * tab[:, :LANES] + kr[:, LANES:] * tab[:, LANES:]
        krot_ref[...] = krot.astype(krot_ref.dtype)


def kv_up(cfg, z, g, k_tab, w):
    n = z.shape[0]
    kd, nout = w.shape
    assert kd % (2 * LANES) == 0
    tm = _pick_tile(n, 512)
    tn = _pick_tile(nout, 1024)
    kr_block = kd // (2 * LANES)
    return pl.pallas_call(
        functools.partial(_kv_up_kernel, eps=cfg.eps),
        grid=(n // tm, nout // tn),
        in_specs=[
            pl.BlockSpec((tm, kd), lambda i, j: (i, 0)),
            pl.BlockSpec((1, kd), lambda i, j: (0, 0)),
            pl.BlockSpec((tm, 2 * LANES), lambda i, j: (i, kr_block)),
            pl.BlockSpec((tm, 2 * LANES), lambda i, j: (i, 0)),
            pl.BlockSpec((kd, tn), lambda i, j: (0, j)),
        ],
        out_specs=[
            pl.BlockSpec((tm, tn), lambda i, j: (i, j)),
            pl.BlockSpec((tm, kd), lambda i, j: (i, 0)),
            pl.BlockSpec((tm, LANES), lambda i, j: (i, 0)),
        ],
        out_shape=[
            jax.ShapeDtypeStruct((n, nout), BF16),
            jax.ShapeDtypeStruct((n, kd), F32),
            jax.ShapeDtypeStruct((n, LANES), BF16),
        ],
        compiler_params=_cparams(("arbitrary", "arbitrary")),
        name="kv_up",
    )(z, g.reshape(1, kd), z, k_tab, w)


def _cache_kv_kernel(x_ref, w_ref, o_ref):
    o_ref[...] = jnp.dot(x_ref[...].astype(BF16), w_ref[...], preferred_element_type=F32).astype(o_ref.dtype)


def cache_kv_up(cfg, ckv, w):
    n, kd = ckv.shape
    nout = w.shape[1]
    tm = _pick_tile(n, 512)
    tn = _pick_tile(nout, 1024)
    return pl.pallas_call(
        _cache_kv_kernel,
        grid=(n // tm, nout // tn),
        in_specs=[
            pl.BlockSpec((tm, kd), lambda i, j: (i, 0)),
            pl.BlockSpec((kd, tn), lambda i, j: (0, j)),
        ],
        out_specs=pl.BlockSpec((tm, tn), lambda i, j: (i, j)),
        out_shape=jax.ShapeDtypeStruct((n, nout), BF16),
        compiler_params=_cparams(("arbitrary", "arbitrary")),
        name="cache_kv_up",
    )(ckv, w)


def _attn_kernel(*refs, hpb, scale, has_cache):
    if has_cache:
        cs_ref, q_ref, kv_ref, kr_ref, kvc_ref, krc_ref, o_ref = refs
    else:
        cs_ref, q_ref, kv_ref, kr_ref, o_ref = refs
    cs = cs_ref[...]
    kr = kr_ref[...]
    nt = (((1,), (1,)), ((), ()))
    for h in range(hpb):
        c0 = h * HEAD_PITCH
        qn = q_ref[:, c0:c0 + LANES]
        qr = (q_ref[:, c0 + LANES:c0 + HEAD_PITCH].astype(F32) * cs).astype(BF16)
        qc = jnp.concatenate([qn, qr], axis=1)
        kc = jnp.concatenate([kv_ref[:, c0:c0 + LANES], kr], axis=1)
        v = kv_ref[:, c0 + LANES:c0 + HEAD_PITCH]
        s = lax.dot_general(qc, kc, nt, preferred_element_type=F32) * scale
        m = jnp.max(s, axis=-1, keepdims=True)
        if has_cache:
            kcc = jnp.concatenate([kvc_ref[:, c0:c0 + LANES], krc_ref[...]], axis=1)
            vc = kvc_ref[:, c0 + LANES:c0 + HEAD_PITCH]
            sc = lax.dot_general(qc, kcc, nt, preferred_element_type=F32) * scale
            m = jnp.maximum(m, jnp.max(sc, axis=-1, keepdims=True))
        p = jnp.exp(s - m)
        l = jnp.sum(p, axis=-1, keepdims=True)
        o = jnp.dot(p.astype(BF16), v, preferred_element_type=F32)
        if has_cache:
            pc = jnp.exp(sc - m)
            l = l + jnp.sum(pc, axis=-1, keepdims=True)
            o = o + jnp.dot(pc.astype(BF16), vc, preferred_element_type=F32)
        o_ref[:, h * V_DIM:(h + 1) * V_DIM] = o / l


def attention(cfg, q, kv, kr, q_tab, *, row0, n_seq, seq_len, tq, hpb, cache=None):
    nh = cfg.n_heads
    assert seq_len % tq == 0 and row0 % seq_len == 0 and nh % hpb == 0
    nq = seq_len // tq
    qb0 = row0 // tq
    sb0 = row0 // seq_len
    scale = float((QK_NOPE + QK_ROPE) ** -0.5)
    in_specs = [
        pl.BlockSpec((tq, LANES), lambda b, h, i: (qb0 + b * nq + i, 0)),
        pl.BlockSpec((tq, hpb * HEAD_PITCH), lambda b, h, i: (qb0 + b * nq + i, h)),
        pl.BlockSpec((seq_len, hpb * HEAD_PITCH), lambda b, h, i: (sb0 + b, h)),
        pl.BlockSpec((seq_len, LANES), lambda b, h, i: (sb0 + b, 0)),
    ]
    args = [q_tab, q, kv, kr]
    if cache is not None:
        kvc, krc = cache
        past = kvc.shape[0] // n_seq
        in_specs += [
            pl.BlockSpec((past, hpb * HEAD_PITCH), lambda b, h, i: (b, h)),
            pl.BlockSpec((past, LANES), lambda b, h, i: (b, 0)),
        ]
        args += [kvc, krc]
    return pl.pallas_call(
        functools.partial(_attn_kernel, hpb=hpb, scale=scale, has_cache=cache is not None),
        grid=(n_seq, nh // hpb, nq),
        in_specs=in_specs,
        out_specs=pl.BlockSpec((tq, hpb * V_DIM), lambda b, h, i: (b * nq + i, h)),
        out_shape=jax.ShapeDtypeStruct((n_seq * seq_len, nh * V_DIM), F32),
        compiler_params=_cparams(("arbitrary", "arbitrary", "arbitrary")),
        name="attention_cached" if cache is not None else "attention",
    )(*args)


def _softplus(x):
    return jnp.maximum(x, 0.0) + jnp.log1p(jnp.exp(-jnp.abs(x)))


def _sigmoid(x):
    return 0.5 * jnp.tanh(0.5 * x) + 0.5


def _sqrt_nonneg(z):
    return jnp.where(z > 0.0, z * lax.rsqrt(z), 0.0)


def _lru_kernel(xb_ref, gb_ref, cw_ref, cb_ref, w_ref, b_ref, lam_ref, h0_ref,
                rec_ref, st_ref, af_ref, uf_ref, ab_ref, ub_ref, *, seq_len, chunk, nslab, lru_c):
    seg = seq_len // SUBLANES
    pitch = seg + SUBLANES
    piece = min(chunk, seg)
    npiece = chunk // piece
    cps = seg // piece
    nchunk = seq_len // chunk
    cw = cw_ref[...]
    cb = cb_ref[...]
    sp = _softplus(-lam_ref[...])
    dirs = ((af_ref, uf_ref), (ab_ref, ub_ref))

    def scratch_row(c, k):
        g = c * npiece + k
        return pl.multiple_of((g // cps) * pitch + (g % cps) * piece, SUBLANES)

    def gate_chunk(c, carry):
        r0 = pl.multiple_of(c * chunk, chunk)
        cur = xb_ref[pl.ds(r0, chunk), :]
        prev_start = pl.multiple_of(jnp.maximum(r0 - SUBLANES, 0), SUBLANES)
        next_start = pl.multiple_of(jnp.minimum(r0 + chunk, seq_len - SUBLANES), SUBLANES)
        prev = jnp.where(c > 0, xb_ref[pl.ds(prev_start, SUBLANES), :], 0.0)
        nxt = jnp.where(c < nchunk - 1, xb_ref[pl.ds(next_start, SUBLANES), :], 0.0)
        ext = jnp.concatenate([prev, cur, nxt], axis=0)
        rows = chunk + 2 * SUBLANES
        xm2 = pltpu.roll(ext, 2, axis=0)[SUBLANES:SUBLANES + chunk]
        xm1 = pltpu.roll(ext, 1, axis=0)[SUBLANES:SUBLANES + chunk]
        xp1 = pltpu.roll(ext, rows - 1, axis=0)[SUBLANES:SUBLANES + chunk]
        xc = cw[0:1] * xm2 + cw[1:2] * xm1 + cw[2:3] * cur + cw[3:4] * xp1 + cb
        for n in range(nslab):
            lo = n * LRU_BLK
            xn = xc[:, lo:lo + LRU_BLK]
            g = jnp.dot(xn.astype(BF16), w_ref[n], preferred_element_type=F32) + b_ref[n]
            for d, (a_ref, u_ref) in enumerate(dirs):
                r = _sigmoid(g[:, (2 * d) * LRU_BLK:(2 * d + 1) * LRU_BLK])
                i = _sigmoid(g[:, (2 * d + 1) * LRU_BLK:(2 * d + 2) * LRU_BLK])
                log_a = (-lru_c) * r * sp[d:d + 1, lo:lo + LRU_BLK]
                t = jnp.tanh(log_a)
                a = jnp.exp(log_a)
                u = _sqrt_nonneg(-2.0 * t / (1.0 - t)) * (i * xn)
                for k in range(npiece):
                    q0 = scratch_row(c, k)
                    a_ref[n, pl.ds(q0, piece), :] = a[k * piece:(k + 1) * piece]
                    u_ref[n, pl.ds(q0, piece), :] = u[k * piece:(k + 1) * piece]
        return carry

    lax.fori_loop(0, nchunk, gate_chunk, 0)

    def local_time(d, s):
        return s if d == 0 else seg - 1 - s

    def strided(ref, n, s):
        return ref.at[n, pl.ds(s, SUBLANES, stride=pitch), :]

    def pass1(s, carry):
        out = []
        for d, (a_ref, u_ref) in enumerate(dirs):
            for n in range(nslab):
                e, p = carry[d * nslab + n]
                a = strided(a_ref, n, local_time(d, s))[...]
                u = strided(u_ref, n, local_time(d, s))[...]
                out.append((a * e + u, a * p))
        return tuple(out)

    zero = jnp.zeros((SUBLANES, LRU_BLK), F32)
    one = jnp.ones((SUBLANES, LRU_BLK), F32)
    ends = lax.fori_loop(0, seg, pass1, tuple((zero, one) for _ in range(2 * nslab)), unroll=4)

    h0 = h0_ref[0]
    entries = []
    finals = [[], []]
    for d in range(2):
        order = range(SUBLANES) if d == 0 else range(SUBLANES - 1, -1, -1)
        for n in range(nslab):
            e, p = ends[d * nslab + n]
            h = h0[d:d + 1, n * LRU_BLK:(n + 1) * LRU_BLK]
            rows = [None] * SUBLANES
            for j in order:
                rows[j] = h
                h = p[j:j + 1] * h + e[j:j + 1]
            entries.append(jnp.concatenate(rows, axis=0))
            finals[d].append(h)
    st_ref[0] = jnp.concatenate([jnp.concatenate(finals[0], axis=1), jnp.concatenate(finals[1], axis=1)], axis=0)

    def pass2(s, carry):
        out = []
        for d, (a_ref, u_ref) in enumerate(dirs):
            for n in range(nslab):
                h = carry[d * nslab + n]
                a = strided(a_ref, n, local_time(d, s))[...]
                u_view = strided(u_ref, n, local_time(d, s))
                h = a * h + u_view[...]
                u_view[...] = h
                out.append(h)
        return tuple(out)

    lax.fori_loop(0, seg, pass2, tuple(entries), unroll=4)

    def out_chunk(c, carry):
        r0 = pl.multiple_of(c * chunk, chunk)
        slabs = []
        for n in range(nslab):
            pieces = []
            for k in range(npiece):
                q0 = scratch_row(c, k)
                pieces.append(uf_ref[n, pl.ds(q0, piece), :] + ub_ref[n, pl.ds(q0, piece), :])
            slabs.append(jnp.concatenate(pieces, axis=0))
        h = jnp.concatenate(slabs, axis=1)
        rec_ref[pl.ds(r0, chunk), :] = h * jax.nn.gelu(gb_ref[pl.ds(r0, chunk), :])
        return carry

    lax.fori_loop(0, nchunk, out_chunk, 0)


def rglru(cfg, z, conv_w, conv_b, wcat, bcat, lam, h0, *, col0, row0, n_seq, seq_len):
    w = cfg.lru_w
    cwid = min(cfg.lru_cw, w)
    nslab = cwid // LRU_BLK
    ncb = w // cwid
    seg = seq_len // SUBLANES
    chunk = min(cfg.lru_chunk, seq_len)
    sb0 = row0 // seq_len
    cb0 = col0 // cwid
    assert row0 % seq_len == 0 and col0 % cwid == 0 and seq_len % chunk == 0 and seg % SUBLANES == 0
    assert chunk % seg == 0 or seg % chunk == 0
    scratch_rows = SUBLANES * (seg + SUBLANES)
    return pl.pallas_call(
        functools.partial(_lru_kernel, seq_len=seq_len, chunk=chunk, nslab=nslab, lru_c=cfg.lru_c),
        grid=(n_seq, ncb),
        in_specs=[
            pl.BlockSpec((seq_len, cwid), lambda b, c: (sb0 + b, cb0 + c)),
            pl.BlockSpec((seq_len, cwid), lambda b, c: (sb0 + b, cb0 + ncb + c)),
            pl.BlockSpec((cfg.conv_w, cwid), lambda b, c: (0, c)),
            pl.BlockSpec((1, cwid), lambda b, c: (0, c)),
            pl.BlockSpec((nslab, LRU_BLK, 4 * LRU_BLK), lambda b, c: (c, 0, 0)),
            pl.BlockSpec((nslab, 1, 4 * LRU_BLK), lambda b, c: (c, 0, 0)),
            pl.BlockSpec((2, cwid), lambda b, c: (0, c)),
            pl.BlockSpec((1, 2, cwid), lambda b, c: (b, 0, c)),
        ],
        out_specs=[
            pl.BlockSpec((seq_len, cwid), lambda b, c: (b, c)),
            pl.BlockSpec((1, 2, cwid), lambda b, c: (b, 0, c)),
        ],
        out_shape=[
            jax.ShapeDtypeStruct((n_seq * seq_len, w), F32),
            jax.ShapeDtypeStruct((n_seq, 2, w), F32),
        ],
        scratch_shapes=[pltpu.VMEM((nslab, scratch_rows, LRU_BLK), F32) for _ in range(4)],
        compiler_params=_cparams(("arbitrary", "arbitrary")),
        name="rglru",
    )(z, z, conv_w, conv_b.reshape(1, w), wcat, bcat, lam, h0)


def _merge_kernel(ac_ref, al_ref, rc_ref, rl_ref, ga_ref, gr_ref, o_ref, *, n_ctx_tiles, eps, wa):
    i = pl.program_id(0)

    def emit(a_ref, r_ref):
        o_ref[:, :wa] = _rms(a_ref[...], ga_ref[...], eps).astype(o_ref.dtype)
        o_ref[:, wa:] = _rms(r_ref[...], gr_ref[...], eps).astype(o_ref.dtype)

    @pl.when(i < n_ctx_tiles)
    def _():
        emit(ac_ref, rc_ref)

    @pl.when(i >= n_ctx_tiles)
    def _():
        emit(al_ref, rl_ref)


def merge_norm(cfg, attn_c, attn_l, rec_c, rec_l, g_mla, g_lru):
    tr = cfg.row_tile
    wa = attn_c.shape[1]
    wr = rec_c.shape[1]
    nct = cfg.n_ctx // tr
    nlt = cfg.n_lat // tr
    ctx_map = lambda i: (jnp.minimum(i, nct - 1), 0)
    lat_map = lambda i: (jnp.maximum(i - nct, 0), 0)
    return pl.pallas_call(
        functools.partial(_merge_kernel, n_ctx_tiles=nct, eps=cfg.eps, wa=wa),
        grid=(nct + nlt,),
        in_specs=[
            pl.BlockSpec((tr, wa), ctx_map),
            pl.BlockSpec((tr, wa), lat_map),
            pl.BlockSpec((tr, wr), ctx_map),
            pl.BlockSpec((tr, wr), lat_map),
            pl.BlockSpec((1, wa), lambda i: (0, 0)),
            pl.BlockSpec((1, wr), lambda i: (0, 0)),
        ],
        out_specs=pl.BlockSpec((tr, wa + wr), lambda i: (i, 0)),
        out_shape=jax.ShapeDtypeStruct((cfg.n_all, wa + wr), BF16),
        compiler_params=_cparams(("arbitrary",)),
        name="merge_norm",
    )(attn_c, attn_l, rec_c, rec_l, g_mla.reshape(1, wa), g_lru.reshape(1, wr))


def _split_bf16(x):
    hi = x.astype(BF16)
    lo = (x - hi.astype(F32)).astype(BF16)
    return hi, lo


def _router_kernel(x_ref, g_ref, sh_ref, sc_ref, wh_ref, wl_ref, y_ref, idx_ref, gate_ref, *, eps, n_experts):
    y = _rms(x_ref[...], g_ref[...], eps)
    y = y * (1.0 + sc_ref[0]) + sh_ref[0]
    y_ref[...] = y
    yh, yl = _split_bf16(y)
    wh = wh_ref[...]
    logits = (jnp.dot(yh, wh, preferred_element_type=F32)
              + jnp.dot(yh, wl_ref[...], preferred_element_type=F32)
              + jnp.dot(yl, wh, preferred_element_type=F32))
    lane = lax.broadcasted_iota(jnp.int32, logits.shape, 1)
    neg = jnp.float32(-jnp.inf)
    l1 = jnp.where(lane < n_experts, logits, neg)
    m1 = jnp.max(l1, axis=-1, keepdims=True)
    i1 = jnp.min(jnp.where(l1 == m1, lane, LANES), axis=-1, keepdims=True)
    l2 = jnp.where(lane == i1, neg, l1)
    m2 = jnp.max(l2, axis=-1, keepdims=True)
    i2 = jnp.min(jnp.where(l2 == m2, lane, LANES), axis=-1, keepdims=True)
    e = jnp.exp(m2 - m1)
    den = 1.0 + e
    idx_ref[...] = jnp.where(lane == 0, i1, jnp.where(lane == 1, i2, 0))
    gate_ref[...] = jnp.where(lane == 0, 1.0 / den, jnp.where(lane == 1, e / den, 0.0))


def router_top2(cfg, x, g, shift, scale, wr_hi, wr_lo):
    n, d = x.shape
    tr = cfg.row_tile
    sel = _mod_sel(cfg, tr)
    return pl.pallas_call(
        functools.partial(_router_kernel, eps=cfg.eps, n_experts=cfg.n_experts),
        grid=(n // tr,),
        in_specs=[
            pl.BlockSpec((tr, d), lambda i: (i, 0)),
            pl.BlockSpec((1, d), lambda i: (0, 0)),
            pl.BlockSpec((1, 1, d), lambda i: (sel(i), 0, 0)),
            pl.BlockSpec((1, 1, d), lambda i: (sel(i), 0, 0)),
            pl.BlockSpec((d, LANES), lambda i: (0, 0)),
            pl.BlockSpec((d, LANES), lambda i: (0, 0)),
        ],
        out_specs=[
            pl.BlockSpec((tr, d), lambda i: (i, 0)),
            pl.BlockSpec((tr, LANES), lambda i: (i, 0)),
            pl.BlockSpec((tr, LANES), lambda i: (i, 0)),
        ],
        out_shape=[
            jax.ShapeDtypeStruct((n, d), F32),
            jax.ShapeDtypeStruct((n, LANES), jnp.int32),
            jax.ShapeDtypeStruct((n, LANES), F32),
        ],
        compiler_params=_cparams(("arbitrary",)),
        name="router_top2",
    )(x, g.reshape(1, d), shift, scale, wr_hi, wr_lo)


def _gather_kernel(idx_ref, x_hbm, o_hbm, sems, *, rows, nsteps):
    i = pl.program_id(0)
    slot = i % 2

    def issue(r, c):
        pltpu.make_async_copy(x_hbm.at[pl.ds(idx_ref[0, 0, r], 1), :],
                              o_hbm.at[pl.ds(i * rows + r, 1), :], sems.at[slot]).start()
        return c

    lax.fori_loop(0, rows, issue, 0, unroll=8)

    def drain(s):
        def wait(r, c):
            pltpu.make_async_copy(x_hbm.at[pl.ds(0, 1), :], o_hbm.at[pl.ds(0, 1), :], sems.at[s]).wait()
            return c
        lax.fori_loop(0, rows, wait, 0, unroll=8)

    @pl.when(i > 0)
    def _():
        drain(1 - slot)

    @pl.when(i == nsteps - 1)
    def _():
        drain(slot)


def gather_rows(cfg, x, idx_sorted):
    n, d = x.shape
    p = idx_sorted.shape[0]
    rows = cfg.gather_rows
    nsteps = p // rows
    return pl.pallas_call(
        functools.partial(_gather_kernel, rows=rows, nsteps=nsteps),
        grid=(nsteps,),
        in_specs=[
            pl.BlockSpec((1, 1, rows), lambda i: (i, 0, 0), memory_space=pltpu.SMEM),
            pl.BlockSpec(memory_space=pl.ANY),
        ],
        out_specs=pl.BlockSpec(memory_space=pl.ANY),
        out_shape=jax.ShapeDtypeStruct((p, d), F32),
        scratch_shapes=[pltpu.SemaphoreType.DMA((2,))],
        compiler_params=_cparams(("arbitrary",)),
        name="moe_gather",
    )(idx_sorted.reshape(nsteps, 1, rows), x)


def _experts_kernel(te_ref, nu_ref, xs_ref, wg_ref, wu_ref, wd_ref, gs_ref, o_ref, xb_ref, h_ref, *, nf, tf):
    i = pl.program_id(0)
    s = pl.program_id(1)
    live = i < nu_ref[0]

    @pl.when(live & (s == 0))
    def _():
        xb_ref[...] = xs_ref[...].astype(BF16)

    @pl.when(live & (s < nf))
    def _():
        x = xb_ref[...]
        g = jnp.dot(x, wg_ref[0], preferred_element_type=F32)
        u = jnp.dot(x, wu_ref[0], preferred_element_type=F32)
        h_ref[s] = (g * jax.nn.sigmoid(g) * u).astype(BF16)

    @pl.when(live & (s >= nf))
    def _():
        acc = jnp.dot(h_ref[0], wd_ref[0, 0:tf, :], preferred_element_type=F32)
        for f in range(1, nf):
            acc += jnp.dot(h_ref[f], wd_ref[0, f * tf:(f + 1) * tf, :], preferred_element_type=F32)
        o_ref[...] = gs_ref[...] * acc

    @pl.when(jnp.logical_not(live) & (s >= nf))
    def _():
        o_ref[...] = jnp.zeros_like(o_ref)


def grouped_experts(cfg, xs, tile_expert, n_used, wg, wu, wd, gate_sorted):
    p, d = xs.shape
    fdim = wg.shape[2]
    tm = cfg.moe_tm
    tf = _pick_tile(fdim, cfg.moe_tf)
    tn = _pick_tile(d, cfg.moe_tn)
    nf = fdim // tf
    nn = d // tn
    nt = p // tm

    def live(i, nu):
        return i < nu[0]

    def row_map(i, s, te, nu):
        return (jnp.minimum(i, nu[0] - 1), 0)

    def w_up_map(i, s, te, nu):
        return (te[i], 0, jnp.where(live(i, nu), jnp.minimum(s, nf - 1), nf - 1))

    def out_col(s):
        return jnp.clip(s - nf, 0, nn - 1)

    def w_dn_map(i, s, te, nu):
        return (te[i], 0, jnp.where(live(i, nu), out_col(s), nn - 1))

    grid_spec = pltpu.PrefetchScalarGridSpec(
        num_scalar_prefetch=2,
        grid=(nt, nf + nn),
        in_specs=[
            pl.BlockSpec((tm, d), row_map),
            pl.BlockSpec((1, d, tf), w_up_map),
            pl.BlockSpec((1, d, tf), w_up_map),
            pl.BlockSpec((1, fdim, tn), w_dn_map),
            pl.BlockSpec((tm, 1), row_map),
        ],
        out_specs=pl.BlockSpec((tm, tn), lambda i, s, te, nu: (i, out_col(s))),
        scratch_shapes=[pltpu.VMEM((tm, d), BF16), pltpu.VMEM((nf, tm, tf), BF16)],
    )
    return pl.pallas_call(
        functools.partial(_experts_kernel, nf=nf, tf=tf),
        grid_spec=grid_spec,
        out_shape=jax.ShapeDtypeStruct((p, d), F32),
        compiler_params=_cparams(("arbitrary", "arbitrary")),
        name="moe_experts",
    )(tile_expert, n_used, xs, wg, wu, wd, gate_sorted.reshape(p, 1))


def _combine_kernel(p1_ref, p2_ref, n1_ref, n2_ref, ys_hbm, x_ref, g_ref, o_ref, buf_ref, sems, *, rows, nsteps):
    i = pl.program_id(0)
    slot = i % 2

    def row_copy(pos_ref, k, r, s):
        return pltpu.make_async_copy(ys_hbm.at[pl.ds(pos_ref[0, 0, r], 1), :],
                                     buf_ref.at[s, k, pl.ds(r, 1), :], sems.at[s])

    def issue(a_ref, b_ref, s):
        def body(r, c):
            row_copy(a_ref, 0, r, s).start()
            row_copy(b_ref, 1, r, s).start()
            return c
        lax.fori_loop(0, rows, body, 0, unroll=8)

    @pl.when(i == 0)
    def _():
        issue(p1_ref, p2_ref, slot)

    @pl.when(i + 1 < nsteps)
    def _():
        issue(n1_ref, n2_ref, 1 - slot)

    def wait(r, c):
        row_copy(p1_ref, 0, r, slot).wait()
        row_copy(p2_ref, 1, r, slot).wait()
        return c

    lax.fori_loop(0, rows, wait, 0, unroll=8)
    o_ref[...] = x_ref[...] + g_ref[0] * (buf_ref[slot, 0] + buf_ref[slot, 1])


def moe_combine(cfg, x, ys, pos1, pos2, gate):
    n, d = x.shape
    rows = cfg.combine_rows
    nsteps = n // rows
    sel = _mod_sel(cfg, rows)
    cur = lambda i: (i, 0, 0)
    nxt = lambda i: (jnp.minimum(i + 1, nsteps - 1), 0, 0)
    p1 = pos1.reshape(nsteps, 1, rows)
    p2 = pos2.reshape(nsteps, 1, rows)
    return pl.pallas_call(
        functools.partial(_combine_kernel, rows=rows, nsteps=nsteps),
        grid=(nsteps,),
        in_specs=[
            pl.BlockSpec((1, 1, rows), cur, memory_space=pltpu.SMEM),
            pl.BlockSpec((1, 1, rows), cur, memory_space=pltpu.SMEM),
            pl.BlockSpec((1, 1, rows), nxt, memory_space=pltpu.SMEM),
            pl.BlockSpec((1, 1, rows), nxt, memory_space=pltpu.SMEM),
            pl.BlockSpec(memory_space=pl.ANY),
            pl.BlockSpec((rows, d), lambda i: (i, 0)),
            pl.BlockSpec((1, 1, d), lambda i: (sel(i), 0, 0)),
        ],
        out_specs=pl.BlockSpec((rows, d), lambda i: (i, 0)),
        out_shape=jax.ShapeDtypeStruct((n, d), F32),
        scratch_shapes=[pltpu.VMEM((2, 2, rows, d), F32), pltpu.SemaphoreType.DMA((2,))],
        compiler_params=_cparams(("arbitrary",)),
        name="moe_combine",
    )(p1, p2, p1, p2, ys, x, gate)


def _routing_tables(cfg, idx, gates):
    n = idx.shape[0]
    ne = cfg.n_experts
    tm = cfg.moe_tm
    p = _round_up(2 * n, tm) + ne * tm
    nt = p // tm
    flat_e = idx[:, :2].reshape(-1)
    flat_g = gates[:, :2].reshape(-1)
    onehot = (flat_e[:, None] == jnp.arange(ne, dtype=jnp.int32)[None, :]).astype(jnp.int32)
    csum = jnp.cumsum(onehot, axis=0)
    rank = jnp.sum(csum * onehot, axis=1) - 1
    counts = csum[-1]
    tiles_e = (counts + tm - 1) // tm
    tiles_end = jnp.cumsum(tiles_e)
    row_off = (tiles_end - tiles_e) * tm
    pos = jnp.sum(onehot * row_off[None, :], axis=1) + rank
    n_used = tiles_end[-1]
    tile_ids = jnp.arange(nt, dtype=jnp.int32)
    tile_expert = jnp.sum((tile_ids[:, None] >= tiles_end[None, :]).astype(jnp.int32), axis=1)
    last_expert = jnp.sum((n_used - 1 >= tiles_end).astype(jnp.int32))
    tile_expert = jnp.where(tile_ids < n_used, tile_expert, last_expert).astype(jnp.int32)
    tok = jnp.arange(2 * n, dtype=jnp.int32) // 2
    idx_sorted = jnp.zeros((p,), jnp.int32).at[pos].set(tok)
    gate_sorted = jnp.zeros((p,), F32).at[pos].set(flat_g)
    pos = pos.reshape(n, 2).astype(jnp.int32)
    return idx_sorted, gate_sorted, pos[:, 0], pos[:, 1], tile_expert, n_used.reshape(1).astype(jnp.int32)


def moe_ffn(cfg, x, g, shift, scale, gate, wr_hi, wr_lo, wg, wu, wd):
    h2, idx, gates = router_top2(cfg, x, g, shift, scale, wr_hi, wr_lo)
    idx_sorted, gate_sorted, pos1, pos2, tile_expert, n_used = _routing_tables(cfg, idx, gates)
    xs = gather_rows(cfg, h2, idx_sorted)
    ys = grouped_experts(cfg, xs, tile_expert, n_used, wg, wu, wd, gate_sorted)
    return moe_combine(cfg, x, ys, pos1, pos2, gate)


def _rope_swap_perm():
    q = QK_ROPE // 4
    return jnp.array(list(range(q, 2 * q)) + list(range(0, q)) + list(range(3 * q, 4 * q)) + list(range(2 * q, 3 * q)))


def _rope_tables(cfg):
    t = jnp.arange(cfg.dec_seq)
    half = QK_ROPE // 2
    inv = cfg.rope_theta ** (-jnp.arange(half // 2, dtype=F32) * 2.0 / half)
    ang_r = (t // cfg.grid_w).astype(F32)[:, None] * inv[None, :]
    ang_c = (t % cfg.grid_w).astype(F32)[:, None] * inv[None, :]
    c = jnp.concatenate([jnp.cos(ang_r), jnp.cos(ang_r), jnp.cos(ang_c), jnp.cos(ang_c)], axis=1)
    s = jnp.concatenate([-jnp.sin(ang_r), jnp.sin(ang_r), -jnp.sin(ang_c), jnp.sin(ang_c)], axis=1)
    c = jnp.concatenate([jnp.ones((cfg.n_ctx, QK_ROPE), F32), jnp.tile(c, (cfg.dec_batch, 1))], axis=0)
    s = jnp.concatenate([jnp.zeros((cfg.n_ctx, QK_ROPE), F32), jnp.tile(s, (cfg.dec_batch, 1))], axis=0)
    return jnp.concatenate([c, s], axis=1), jnp.concatenate([c, c, s, s], axis=1)


def _prep_layer_weights(cfg, w_in, w_q_up, lru_w_a, lru_w_x, lru_b_a, lru_b_x):
    ql, kl, w = cfg.q_lora, cfg.kv_lora, cfg.lru_w
    perm = _rope_swap_perm()
    o_kv = ql
    o_kr = ql + kl
    o_xb = o_kr + QK_ROPE
    wq_c = w_in[:, :, :ql]
    wkv_c = w_in[:, :, o_kv:o_kr]
    wkr = w_in[:, :, o_kr:o_xb]
    wkr_sw = wkr[:, :, perm]
    w_all = jnp.concatenate([wkv_c, wkr, wkr, wkr_sw, wkr_sw, wq_c, w_in[:, :, o_xb:]], axis=2).astype(BF16)
    depth = w_q_up.shape[0]
    wq = w_q_up.reshape(depth, ql, cfg.n_heads, QK_NOPE + QK_ROPE)
    wq_rope = wq[..., QK_NOPE:]
    wq = jnp.concatenate([wq[..., :QK_NOPE], wq_rope, wq_rope[..., perm]], axis=-1)
    wq = wq.reshape(depth, ql, cfg.n_heads * HEAD_PITCH).astype(BF16)
    wcat = jnp.concatenate([lru_w_a[:, 0], lru_w_x[:, 0], lru_w_a[:, 1], lru_w_x[:, 1]], axis=-1).astype(BF16)
    nb = w // LRU_BLK
    bsplit = lambda b: b.reshape(depth, 2, nb, 1, LRU_BLK)
    ba, bx = bsplit(lru_b_a), bsplit(lru_b_x)
    bcat = jnp.concatenate([ba[:, 0], bx[:, 0], ba[:, 1], bx[:, 1]], axis=-1)
    return w_all, wq, wcat, bcat


def _pad_ff(w, axis, to):
    pad = [(0, 0)] * w.ndim
    pad[axis] = (0, to - w.shape[axis])
    return jnp.pad(w, pad)


def _forward(cfg, x_prompt, x_sample, c, cache_ckv, cache_k_rope, state_lru, c_ctx, w_ada, b_ada,
             norm_mix, norm_ffn, w_in, q_a_norm, w_q_up, kv_a_norm, w_kv_up, conv_w, conv_b,
             lru_w_a, lru_b_a, lru_w_x, lru_b_x, lru_lambda, out_norm_mla, out_norm_lru, w_out,
             ffn_w_gate, ffn_w_up, ffn_w_down, router_w, exp_w_gate, exp_w_up, exp_w_down,
             final_norm_g):
    d = cfg.d_model
    n_ctx, n_lat = cfg.n_ctx, cfg.n_lat
    n_mod = 1 + cfg.dec_batch
    assert n_mod <= SUBLANES

    x = jnp.concatenate([x_prompt.reshape(n_ctx, d), x_sample.reshape(n_lat, d)], axis=0)

    cond8 = jnp.zeros((SUBLANES, d), F32).at[0].set(c_ctx).at[1:n_mod].set(c)
    mods = ada_modulation(cond8, w_ada, b_ada)
    mods = mods.reshape(cfg.depth, SUBLANES, 6, 1, d)

    w_all, wq, wcat, bcat = _prep_layer_weights(cfg, w_in, w_q_up, lru_w_a, lru_w_x, lru_b_a, lru_b_x)
    lru_col0 = cfg.kv_lora + 2 * LANES + cfg.q_lora
    q_col_block = (cfg.kv_lora + 2 * LANES) // cfg.q_lora
    assert (cfg.kv_lora + 2 * LANES) % cfg.q_lora == 0
    w_kv = w_kv_up.astype(BF16)
    w_o = w_out.astype(BF16)
    ff_pad = _round_up(cfg.d_ff, cfg.ff_align)
    ffg = _pad_ff(ffn_w_gate, 2, ff_pad).astype(BF16)
    ffu = _pad_ff(ffn_w_up, 2, ff_pad).astype(BF16)
    ffd = _pad_ff(ffn_w_down, 1, ff_pad).astype(BF16)
    exg = exp_w_gate.astype(BF16)
    exu = exp_w_up.astype(BF16)
    exd = exp_w_down.astype(BF16)
    wr = _pad_ff(router_w, 2, LANES)
    wr_hi = wr.astype(BF16)
    wr_lo = (wr - wr_hi.astype(F32)).astype(BF16)

    q_tab, k_tab = _rope_tables(cfg)
    h0_ctx = jnp.zeros((cfg.batch, 2, cfg.lru_w), F32)
    krc_all = jnp.concatenate([cache_k_rope, cache_k_rope], axis=-1).astype(BF16)

    ckv_list, krope_list, lru_list = [], [], []
    for l in range(cfg.depth):
        shift_m, scale_m, gate_m, shift_f, scale_f, gate_f = (mods[l, :, k] for k in range(6))

        h1 = adaln(cfg, x, norm_mix[l], shift_m, scale_m)
        z = matmul(cfg, h1, w_all[l], F32, tm=cfg.mm_tm, tn=cfg.mm_tn)
        q = q_up(cfg, z, q_a_norm[l], wq[l], col_block=q_col_block)
        kv, ckv, krot = kv_up(cfg, z, kv_a_norm[l], k_tab, w_kv[l])
        kvc = cache_kv_up(cfg, cache_ckv[:, l].reshape(cfg.dec_batch * cfg.past_len, cfg.kv_lora), w_kv[l])
        krc = krc_all[:, l].reshape(cfg.dec_batch * cfg.past_len, LANES)

        attn_c = attention(cfg, q, kv, krot, q_tab, row0=0, n_seq=cfg.batch, seq_len=cfg.seq,
                           tq=cfg.seq, hpb=cfg.n_heads)
        attn_l = attention(cfg, q, kv, krot, q_tab, row0=n_ctx, n_seq=cfg.dec_batch, seq_len=cfg.dec_seq,
                           tq=min(cfg.attn_tq, cfg.dec_seq), hpb=cfg.attn_hpb, cache=(kvc, krc))
        lam = lru_lambda[l]
        rec_c, st_c = rglru(cfg, z, conv_w[l], conv_b[l], wcat[l], bcat[l], lam, h0_ctx,
                            col0=lru_col0, row0=0, n_seq=cfg.batch, seq_len=cfg.seq)
        rec_l, _ = rglru(cfg, z, conv_w[l], conv_b[l], wcat[l], bcat[l], lam, state_lru[:, l],
                         col0=lru_col0, row0=n_ctx, n_seq=cfg.dec_batch, seq_len=cfg.dec_seq)
        merged = merge_norm(cfg, attn_c, attn_l, rec_c, rec_l, out_norm_mla[l], out_norm_lru[l])
        x = matmul(cfg, merged, w_o[l], F32, tm=cfg.mm_tm, tn=cfg.mm_tn, resid=x, gate=gate_m)

        ckv_list.append(ckv[:n_ctx].reshape(cfg.batch, cfg.seq, cfg.kv_lora))
        krope_list.append(z[:n_ctx, cfg.kv_lora:cfg.kv_lora + QK_ROPE].reshape(cfg.batch, cfg.seq, QK_ROPE))
        lru_list.append(st_c)

        j = l // 2
        if l % 2 == 0:
            h2 = adaln(cfg, x, norm_ffn[l], shift_f, scale_f)
            hh = swiglu_up(cfg, h2, ffg[j], ffu[j])
            x = matmul(cfg, hh, ffd[j], F32, tm=cfg.down_tm, tn=cfg.mm_tn, resid=x, gate=gate_f)
        else:
            x = moe_ffn(cfg, x, norm_ffn[l], shift_f, scale_f, gate_f, wr_hi[j], wr_lo[j],
                        exg[j], exu[j], exd[j])

    y_prompt = final_norm(cfg, x, final_norm_g, 0, n_ctx).reshape(cfg.batch, cfg.seq, d)
    y_sample = final_norm(cfg, x, final_norm_g, n_ctx, n_lat).reshape(cfg.dec_batch, cfg.dec_seq, d)
    new_ckv = jnp.stack(ckv_list, axis=1)
    new_k_rope = jnp.stack(krope_list, axis=1)
    new_lru_state = jnp.stack(lru_list, axis=1)
    return (y_prompt, y_sample, new_ckv, new_k_rope, new_lru_state)


def kernel(x_prompt, x_sample, c, cache_ckv, cache_k_rope, state_lru, c_ctx, w_ada, b_ada, norm_mix, norm_ffn, w_in, q_a_norm, w_q_up, kv_a_norm, w_kv_up, conv_w, conv_b, lru_w_a, lru_b_a, lru_w_x, lru_b_x, lru_lambda, out_norm_mla, out_norm_lru, w_out, ffn_w_gate, ffn_w_up, ffn_w_down, router_w, exp_w_gate, exp_w_up, exp_w_down, final_norm):
    return _forward(Cfg(), x_prompt, x_sample, c, cache_ckv, cache_k_rope, state_lru, c_ctx, w_ada, b_ada,
                    norm_mix, norm_ffn, w_in, q_a_norm, w_q_up, kv_a_norm, w_kv_up, conv_w, conv_b,
                    lru_w_a, lru_b_a, lru_w_x, lru_b_x, lru_lambda, out_norm_mla, out_norm_lru, w_out,
                    ffn_w_gate, ffn_w_up, ffn_w_down, router_w, exp_w_gate, exp_w_up, exp_w_down,
                    final_norm)
```

```python
import functools
from typing import NamedTuple

import jax
import jax.numpy as jnp
from jax import lax
from jax.experimental import pallas as pl
from jax.experimental.pallas import tpu as pltpu

F32 = jnp.float32
BF16 = jnp.bfloat16

LANES = 128
SUBLANES = 8
VMEM_LIMIT_BYTES = 56 * 1024 * 1024

QK_NOPE = 128
QK_ROPE = 64
V_DIM = 128
LRU_BLK = 128
HEAD_PITCH = 2 * LANES


class Cfg(NamedTuple):
    d_model: int = 4096
    batch: int = 32
    seq: int = 256
    depth: int = 4
    dec_batch: int = 2
    dec_seq: int = 2048
    past_len: int = 256
    grid_w: int = 64
    n_heads: int = 16
    q_lora: int = 768
    kv_lora: int = 512
    lru_w: int = 2048
    conv_w: int = 4
    lru_c: float = 8.0
    d_ff: int = 11008
    n_experts: int = 8
    d_expert: int = 2048
    eps: float = 1e-6
    rope_theta: float = 10000.0
    mm_tm: int = 1024
    mm_tn: int = 512
    down_tm: int = 512
    up_tm: int = 256
    up_tn: int = 4096
    ff_align: int = 1024
    row_tile: int = 256
    attn_tq: int = 256
    attn_hpb: int = 2
    moe_tm: int = 1024
    moe_tf: int = 512
    moe_tn: int = 512
    gather_rows: int = 512
    combine_rows: int = 256
    lru_cw: int = 256
    lru_chunk: int = 256

    @property
    def n_ctx(self):
        return self.batch * self.seq

    @property
    def n_lat(self):
        return self.dec_batch * self.dec_seq

    @property
    def n_all(self):
        return self.n_ctx + self.n_lat


def _cparams(sem):
    return pltpu.CompilerParams(dimension_semantics=sem, vmem_limit_bytes=VMEM_LIMIT_BYTES)


def _pick_tile(n, pref):
    t = pref
    while n % t:
        t //= 2
    assert t >= LANES or t == n, (n, pref)
    return t


def _round_up(n, m):
    return (n + m - 1) // m * m


def _mod_sel(cfg, rows_per_tile):
    def sel(i):
        row0 = i * rows_per_tile
        return jnp.where(row0 < cfg.n_ctx, 0, 1 + (row0 - cfg.n_ctx) // cfg.dec_seq)
    return sel


def _ada_kernel(c_ref, w_ref, b_ref, o_ref):
    c = c_ref[...]
    a = (c * jax.nn.sigmoid(c)).astype(BF16)
    o_ref[0] = jnp.dot(a, w_ref[0].astype(BF16), preferred_element_type=F32) + b_ref[0]


def ada_modulation(cond8, w_ada, b_ada):
    depth, d, n = w_ada.shape
    tn = _pick_tile(n, 512)
    return pl.pallas_call(
        _ada_kernel,
        grid=(depth, n // tn),
        in_specs=[
            pl.BlockSpec((SUBLANES, d), lambda l, j: (0, 0)),
            pl.BlockSpec((1, d, tn), lambda l, j: (l, 0, j)),
            pl.BlockSpec((1, 1, tn), lambda l, j: (l, 0, j)),
        ],
        out_specs=pl.BlockSpec((1, SUBLANES, tn), lambda l, j: (l, 0, j)),
        out_shape=jax.ShapeDtypeStruct((depth, SUBLANES, n), F32),
        compiler_params=_cparams(("arbitrary", "arbitrary")),
        name="ada_modulation",
    )(cond8, w_ada, b_ada.reshape(depth, 1, n))


def _rms(x, g, eps):
    return x * lax.rsqrt(jnp.mean(x * x, axis=-1, keepdims=True) + eps) * g


def _adaln_kernel(x_ref, g_ref, sh_ref, sc_ref, o_ref, *, eps):
    y = _rms(x_ref[...], g_ref[...], eps)
    o_ref[...] = (y * (1.0 + sc_ref[0]) + sh_ref[0]).astype(o_ref.dtype)


def adaln(cfg, x, g, shift, scale):
    n, d = x.shape
    tr = cfg.row_tile
    sel = _mod_sel(cfg, tr)
    return pl.pallas_call(
        functools.partial(_adaln_kernel, eps=cfg.eps),
        grid=(n // tr,),
        in_specs=[
            pl.BlockSpec((tr, d), lambda i: (i, 0)),
            pl.BlockSpec((1, d), lambda i: (0, 0)),
            pl.BlockSpec((1, 1, d), lambda i: (sel(i), 0, 0)),
            pl.BlockSpec((1, 1, d), lambda i: (sel(i), 0, 0)),
        ],
        out_specs=pl.BlockSpec((tr, d), lambda i: (i, 0)),
        out_shape=jax.ShapeDtypeStruct((n, d), BF16),
        compiler_params=_cparams(("arbitrary",)),
        name="adaln",
    )(x, g.reshape(1, d), shift, scale)


def _final_norm_kernel(x_ref, g_ref, o_ref, *, eps):
    o_ref[...] = _rms(x_ref[...], g_ref[...], eps)


def final_norm(cfg, x, g, row0, nrows):
    d = x.shape[1]
    tr = cfg.row_tile
    b0 = row0 // tr
    return pl.pallas_call(
        functools.partial(_final_norm_kernel, eps=cfg.eps),
        grid=(nrows // tr,),
        in_specs=[
            pl.BlockSpec((tr, d), lambda i: (b0 + i, 0)),
            pl.BlockSpec((1, d), lambda i: (0, 0)),
        ],
        out_specs=pl.BlockSpec((tr, d), lambda i: (i, 0)),
        out_shape=jax.ShapeDtypeStruct((nrows, d), F32),
        compiler_params=_cparams(("arbitrary",)),
        name="final_norm",
    )(x, g.reshape(1, d))


def _mm_kernel(*refs, gated):
    if gated:
        x_ref, w_ref, r_ref, g_ref, o_ref = refs
    else:
        x_ref, w_ref, o_ref = refs
    acc = jnp.dot(x_ref[...], w_ref[...], preferred_element_type=F32)
    if gated:
        o_ref[...] = r_ref[...] + g_ref[0] * acc
    else:
        o_ref[...] = acc.astype(o_ref.dtype)


def matmul(cfg, x, w, out_dtype, *, tm, tn, resid=None, gate=None):
    m, kd = x.shape
    n = w.shape[1]
    tm = _pick_tile(m, tm)
    tn = _pick_tile(n, tn)
    gated = resid is not None
    in_specs = [
        pl.BlockSpec((tm, kd), lambda i, j: (i, 0)),
        pl.BlockSpec((kd, tn), lambda i, j: (0, j)),
    ]
    args = [x, w]
    if gated:
        sel = _mod_sel(cfg, tm)
        in_specs += [
            pl.BlockSpec((tm, tn), lambda i, j: (i, j)),
            pl.BlockSpec((1, 1, tn), lambda i, j: (sel(i), 0, j)),
        ]
        args += [resid, gate]
    return pl.pallas_call(
        functools.partial(_mm_kernel, gated=gated),
        grid=(m // tm, n // tn),
        in_specs=in_specs,
        out_specs=pl.BlockSpec((tm, tn), lambda i, j: (i, j)),
        out_shape=jax.ShapeDtypeStruct((m, n), out_dtype),
        compiler_params=_cparams(("arbitrary", "arbitrary")),
        name="matmul_gated" if gated else "matmul",
    )(*args)


def _swiglu_up_kernel(x_ref, wg_ref, wu_ref, o_ref):
    x = x_ref[...]
    g = jnp.dot(x, wg_ref[...], preferred_element_type=F32)
    u = jnp.dot(x, wu_ref[...], preferred_element_type=F32)
    o_ref[...] = (g * jax.nn.sigmoid(g) * u).astype(o_ref.dtype)


def swiglu_up(cfg, x, wg, wu):
    m, kd = x.shape
    n = wg.shape[1]
    tm = _pick_tile(m, cfg.mm_tm)
    tn = _pick_tile(n, cfg.mm_tn)
    return pl.pallas_call(
        _swiglu_up_kernel,
        grid=(m // tm, n // tn),
        in_specs=[
            pl.BlockSpec((tm, kd), lambda i, j: (i, 0)),
            pl.BlockSpec((kd, tn), lambda i, j: (0, j)),
            pl.BlockSpec((kd, tn), lambda i, j: (0, j)),
        ],
        out_specs=pl.BlockSpec((tm, tn), lambda i, j: (i, j)),
        out_shape=jax.ShapeDtypeStruct((m, n), BF16),
        compiler_params=_cparams(("arbitrary", "arbitrary")),
        name="swiglu_up",
    )(x, wg, wu)


def _q_up_kernel(x_ref, g_ref, w_ref, o_ref, *, eps):
    y = _rms(x_ref[...], g_ref[...], eps).astype(BF16)
    o_ref[...] = jnp.dot(y, w_ref[...], preferred_element_type=F32).astype(o_ref.dtype)


def q_up(cfg, z, g, w, col_block):
    n = z.shape[0]
    kd, nout = w.shape
    tm = _pick_tile(n, cfg.up_tm)
    tn = _pick_tile(nout, cfg.up_tn)
    return pl.pallas_call(
        functools.partial(_q_up_kernel, eps=cfg.eps),
        grid=(n // tm, nout // tn),
        in_specs=[
            pl.BlockSpec((tm, kd), lambda i, j: (i, col_block)),
            pl.BlockSpec((1, kd), lambda i, j: (0, 0)),
            pl.BlockSpec((kd, tn), lambda i, j: (0, j)),
        ],
        out_specs=pl.BlockSpec((tm, tn), lambda i, j: (i, j)),
        out_shape=jax.ShapeDtypeStruct((n, nout), BF16),
        compiler_params=_cparams(("arbitrary", "arbitrary")),
        name="q_up",
    )(z, g.reshape(1, kd), w)


def _kv_up_kernel(x_ref, g_ref, kr_ref, tab_ref, w_ref, kv_ref, ckv_ref, krot_ref, *, eps):
    j = pl.program_id(1)
    ckv = _rms(x_ref[...], g_ref[...], eps)
    kv_ref[...] = jnp.dot(ckv.astype(BF16), w_ref[...], preferred_element_type=F32).astype(kv_ref.dtype)

    @pl.when(j == 0)
    def _():
        ckv_ref[...] = ckv
        kr = kr_ref[...]
        tab = tab_ref[...]
        krot = kr[:, :LANES] * tab[:, :LANES] + kr[:, LANES:] * tab[:, LANES:]
        krot_ref[...] = krot.astype(krot_ref.dtype)


def kv_up(cfg, z, g, k_tab, w):
    n = z.shape[0]
    kd, nout = w.shape
    assert kd % (2 * LANES) == 0
    tm = _pick_tile(n, cfg.up_tm)
    tn = _pick_tile(nout, cfg.up_tn)
    kr_block = kd // (2 * LANES)
    return pl.pallas_call(
        functools.partial(_kv_up_kernel, eps=cfg.eps),
        grid=(n // tm, nout // tn),
        in_specs=[
            pl.BlockSpec((tm, kd), lambda i, j: (i, 0)),
            pl.BlockSpec((1, kd), lambda i, j: (0, 0)),
            pl.BlockSpec((tm, 2 * LANES), lambda i, j: (i, kr_block)),
            pl.BlockSpec((tm, 2 * LANES), lambda i, j: (i, 0)),
            pl.BlockSpec((kd, tn), lambda i, j: (0, j)),
        ],
        out_specs=[
            pl.BlockSpec((tm, tn), lambda i, j: (i, j)),
            pl.BlockSpec((tm, kd), lambda i, j: (i, 0)),
            pl.BlockSpec((tm, LANES), lambda i, j: (i, 0)),
        ],
        out_shape=[
            jax.ShapeDtypeStruct((n, nout), BF16),
            jax.ShapeDtypeStruct((n, kd), F32),
            jax.ShapeDtypeStruct((n, LANES), BF16),
        ],
        compiler_params=_cparams(("arbitrary", "arbitrary")),
        name="kv_up",
    )(z, g.reshape(1, kd), z, k_tab, w)


def _cache_kv_kernel(x_ref, w_ref, o_ref):
    o_ref[...] = jnp.dot(x_ref[...].astype(BF16), w_ref[...], preferred_element_type=F32).astype(o_ref.dtype)


def cache_kv_up(cfg, ckv, w):
    n, kd = ckv.shape
    nout = w.shape[1]
    tm = _pick_tile(n, cfg.up_tm)
    tn = _pick_tile(nout, cfg.up_tn)
    return pl.pallas_call(
        _cache_kv_kernel,
        grid=(n // tm, nout // tn),
        in_specs=[
            pl.BlockSpec((tm, kd), lambda i, j: (i, 0)),
            pl.BlockSpec((kd, tn), lambda i, j: (0, j)),
        ],
        out_specs=pl.BlockSpec((tm, tn), lambda i, j: (i, j)),
        out_shape=jax.ShapeDtypeStruct((n, nout), BF16),
        compiler_params=_cparams(("arbitrary", "arbitrary")),
        name="cache_kv_up",
    )(ckv, w)


def _attn_kernel(*refs, hpb, scale, has_cache):
    if has_cache:
        cs_ref, q_ref, kv_ref, kr_ref, kvc_ref, krc_ref, o_ref = refs
    else:
        cs_ref, q_ref, kv_ref, kr_ref, o_ref = refs
    cs = cs_ref[...]
    kr = kr_ref[...]
    nt = (((1,), (1,)), ((), ()))
    for h in range(hpb):
        c0 = h * HEAD_PITCH
        qn = q_ref[:, c0:c0 + LANES]
        qr = (q_ref[:, c0 + LANES:c0 + HEAD_PITCH].astype(F32) * cs).astype(BF16)
        qc = jnp.concatenate([qn, qr], axis=1)
        kc = jnp.concatenate([kv_ref[:, c0:c0 + LANES], kr], axis=1)
        v = kv_ref[:, c0 + LANES:c0 + HEAD_PITCH]
        s = lax.dot_general(qc, kc, nt, preferred_element_type=F32) * scale
        m = jnp.max(s, axis=-1, keepdims=True)
        if has_cache:
            kcc = jnp.concatenate([kvc_ref[:, c0:c0 + LANES], krc_ref[...]], axis=1)
            vc = kvc_ref[:, c0 + LANES:c0 + HEAD_PITCH]
            sc = lax.dot_general(qc, kcc, nt, preferred_element_type=F32) * scale
            m = jnp.maximum(m, jnp.max(sc, axis=-1, keepdims=True))
        p = jnp.exp(s - m)
        l = jnp.sum(p, axis=-1, keepdims=True)
        o = jnp.dot(p.astype(BF16), v, preferred_element_type=F32)
        if has_cache:
            pc = jnp.exp(sc - m)
            l = l + jnp.sum(pc, axis=-1, keepdims=True)
            o = o + jnp.dot(pc.astype(BF16), vc, preferred_element_type=F32)
        o_ref[:, h * V_DIM:(h + 1) * V_DIM] = o / l


def attention(cfg, q, kv, kr, q_tab, *, row0, n_seq, seq_len, tq, hpb, cache=None):
    nh = cfg.n_heads
    assert seq_len % tq == 0 and row0 % seq_len == 0 and nh % hpb == 0
    nq = seq_len // tq
    qb0 = row0 // tq
    sb0 = row0 // seq_len
    scale = float((QK_NOPE + QK_ROPE) ** -0.5)
    in_specs = [
        pl.BlockSpec((tq, LANES), lambda b, h, i: (qb0 + b * nq + i, 0)),
        pl.BlockSpec((tq, hpb * HEAD_PITCH), lambda b, h, i: (qb0 + b * nq + i, h)),
        pl.BlockSpec((seq_len, hpb * HEAD_PITCH), lambda b, h, i: (sb0 + b, h)),
        pl.BlockSpec((seq_len, LANES), lambda b, h, i: (sb0 + b, 0)),
    ]
    args = [q_tab, q, kv, kr]
    if cache is not None:
        kvc, krc = cache
        past = kvc.shape[0] // n_seq
        in_specs += [
            pl.BlockSpec((past, hpb * HEAD_PITCH), lambda b, h, i: (b, h)),
            pl.BlockSpec((past, LANES), lambda b, h, i: (b, 0)),
        ]
        args += [kvc, krc]
    return pl.pallas_call(
        functools.partial(_attn_kernel, hpb=hpb, scale=scale, has_cache=cache is not None),
        grid=(n_seq, nh // hpb, nq),
        in_specs=in_specs,
        out_specs=pl.BlockSpec((tq, hpb * V_DIM), lambda b, h, i: (b * nq + i, h)),
        out_shape=jax.ShapeDtypeStruct((n_seq * seq_len, nh * V_DIM), F32),
        compiler_params=_cparams(("arbitrary", "arbitrary", "arbitrary")),
        name="attention_cached" if cache is not None else "attention",
    )(*args)


def _softplus(x):
    return jnp.maximum(x, 0.0) + jnp.log1p(jnp.exp(-jnp.abs(x)))


def _sigmoid(x):
    return 0.5 * jnp.tanh(0.5 * x) + 0.5


def _sqrt_nonneg(z):
    return jnp.where(z > 0.0, z * lax.rsqrt(z), 0.0)


def _lru_kernel(xb_ref, gb_ref, cw_ref, cb_ref, w_ref, b_ref, lam_ref, h0_ref,
                rec_ref, st_ref, af_ref, uf_ref, ab_ref, ub_ref, *, seq_len, chunk, nslab, lru_c):
    seg = seq_len // SUBLANES
    pitch = seg + SUBLANES
    piece = min(chunk, seg)
    npiece = chunk // piece
    cps = seg // piece
    nchunk = seq_len // chunk
    cw = cw_ref[...]
    cb = cb_ref[...]
    sp = _softplus(-lam_ref[...])
    dirs = ((af_ref, uf_ref), (ab_ref, ub_ref))

    def scratch_row(c, k):
        g = c * npiece + k
        return pl.multiple_of((g // cps) * pitch + (g % cps) * piece, SUBLANES)

    def gate_chunk(c, carry):
        r0 = pl.multiple_of(c * chunk, chunk)
        cur = xb_ref[pl.ds(r0, chunk), :]
        prev_start = pl.multiple_of(jnp.maximum(r0 - SUBLANES, 0), SUBLANES)
        next_start = pl.multiple_of(jnp.minimum(r0 + chunk, seq_len - SUBLANES), SUBLANES)
        prev = jnp.where(c > 0, xb_ref[pl.ds(prev_start, SUBLANES), :], 0.0)
        nxt = jnp.where(c < nchunk - 1, xb_ref[pl.ds(next_start, SUBLANES), :], 0.0)
        ext = jnp.concatenate([prev, cur, nxt], axis=0)
        rows = chunk + 2 * SUBLANES
        xm2 = pltpu.roll(ext, 2, axis=0)[SUBLANES:SUBLANES + chunk]
        xm1 = pltpu.roll(ext, 1, axis=0)[SUBLANES:SUBLANES + chunk]
        xp1 = pltpu.roll(ext, rows - 1, axis=0)[SUBLANES:SUBLANES + chunk]
        xc = cw[0:1] * xm2 + cw[1:2] * xm1 + cw[2:3] * cur + cw[3:4] * xp1 + cb
        for n in range(nslab):
            lo = n * LRU_BLK
            xn = xc[:, lo:lo + LRU_BLK]
            g = jnp.dot(xn.astype(BF16), w_ref[n], preferred_element_type=F32) + b_ref[n]
            for d, (a_ref, u_ref) in enumerate(dirs):
                r = _sigmoid(g[:, (2 * d) * LRU_BLK:(2 * d + 1) * LRU_BLK])
                i = _sigmoid(g[:, (2 * d + 1) * LRU_BLK:(2 * d + 2) * LRU_BLK])
                log_a = (-lru_c) * r * sp[d:d + 1, lo:lo + LRU_BLK]
                t = jnp.tanh(log_a)
                a = jnp.exp(log_a)
                u = _sqrt_nonneg(-2.0 * t / (1.0 - t)) * (i * xn)
                for k in range(npiece):
                    q0 = scratch_row(c, k)
                    a_ref[n, pl.ds(q0, piece), :] = a[k * piece:(k + 1) * piece]
                    u_ref[n, pl.ds(q0, piece), :] = u[k * piece:(k + 1) * piece]
        return carry

    lax.fori_loop(0, nchunk, gate_chunk, 0)

    def local_time(d, s):
        return s if d == 0 else seg - 1 - s

    def strided(ref, n, s):
        return ref.at[n, pl.ds(s, SUBLANES, stride=pitch), :]

    def pass1(s, carry):
        out = []
        for d, (a_ref, u_ref) in enumerate(dirs):
            for n in range(nslab):
                e, p = carry[d * nslab + n]
                a = strided(a_ref, n, local_time(d, s))[...]
                u = strided(u_ref, n, local_time(d, s))[...]
                out.append((a * e + u, a * p))
        return tuple(out)

    zero = jnp.zeros((SUBLANES, LRU_BLK), F32)
    one = jnp.ones((SUBLANES, LRU_BLK), F32)
    ends = lax.fori_loop(0, seg, pass1, tuple((zero, one) for _ in range(2 * nslab)), unroll=4)

    h0 = h0_ref[0]
    entries = []
    finals = [[], []]
    for d in range(2):
        order = range(SUBLANES) if d == 0 else range(SUBLANES - 1, -1, -1)
        for n in range(nslab):
            e, p = ends[d * nslab + n]
            h = h0[d:d + 1, n * LRU_BLK:(n + 1) * LRU_BLK]
            rows = [None] * SUBLANES
            for j in order:
                rows[j] = h
                h = p[j:j + 1] * h + e[j:j + 1]
            entries.append(jnp.concatenate(rows, axis=0))
            finals[d].append(h)
    st_ref[0] = jnp.concatenate([jnp.concatenate(finals[0], axis=1), jnp.concatenate(finals[1], axis=1)], axis=0)

    def pass2(s, carry):
        out = []
        for d, (a_ref, u_ref) in enumerate(dirs):
            for n in range(nslab):
                h = carry[d * nslab + n]
                a = strided(a_ref, n, local_time(d, s))[...]
                u_view = strided(u_ref, n, local_time(d, s))
                h = a * h + u_view[...]
                u_view[...] = h
                out.append(h)
        return tuple(out)

    lax.fori_loop(0, seg, pass2, tuple(entries), unroll=4)

    def out_chunk(c, carry):
        r0 = pl.multiple_of(c * chunk, chunk)
        slabs = []
        for n in range(nslab):
            pieces = []
            for k in range(npiece):
                q0 = scratch_row(c, k)
                pieces.append(uf_ref[n, pl.ds(q0, piece), :] + ub_ref[n, pl.ds(q0, piece), :])
            slabs.append(jnp.concatenate(pieces, axis=0))
        h = jnp.concatenate(slabs, axis=1)
        rec_ref[pl.ds(r0, chunk), :] = h * jax.nn.gelu(gb_ref[pl.ds(r0, chunk), :])
        return carry

    lax.fori_loop(0, nchunk, out_chunk, 0)


def rglru(cfg, z, conv_w, conv_b, wcat, bcat, lam, h0, *, col0, row0, n_seq, seq_len):
    w = cfg.lru_w
    cwid = min(cfg.lru_cw, w)
    nslab = cwid // LRU_BLK
    ncb = w // cwid
    seg = seq_len // SUBLANES
    chunk = min(cfg.lru_chunk, seq_len)
    sb0 = row0 // seq_len
    cb0 = col0 // cwid
    assert row0 % seq_len == 0 and col0 % cwid == 0 and seq_len % chunk == 0 and seg % SUBLANES == 0
    assert chunk % seg == 0 or seg % chunk == 0
    scratch_rows = SUBLANES * (seg + SUBLANES)
    return pl.pallas_call(
        functools.partial(_lru_kernel, seq_len=seq_len, chunk=chunk, nslab=nslab, lru_c=cfg.lru_c),
        grid=(n_seq, ncb),
        in_specs=[
            pl.BlockSpec((seq_len, cwid), lambda b, c: (sb0 + b, cb0 + c)),
            pl.BlockSpec((seq_len, cwid), lambda b, c: (sb0 + b, cb0 + ncb + c)),
            pl.BlockSpec((cfg.conv_w, cwid), lambda b, c: (0, c)),
            pl.BlockSpec((1, cwid), lambda b, c: (0, c)),
            pl.BlockSpec((nslab, LRU_BLK, 4 * LRU_BLK), lambda b, c: (c, 0, 0)),
            pl.BlockSpec((nslab, 1, 4 * LRU_BLK), lambda b, c: (c, 0, 0)),
            pl.BlockSpec((2, cwid), lambda b, c: (0, c)),
            pl.BlockSpec((1, 2, cwid), lambda b, c: (b, 0, c)),
        ],
        out_specs=[
            pl.BlockSpec((seq_len, cwid), lambda b, c: (b, c)),
            pl.BlockSpec((1, 2, cwid), lambda b, c: (b, 0, c)),
        ],
        out_shape=[
            jax.ShapeDtypeStruct((n_seq * seq_len, w), F32),
            jax.ShapeDtypeStruct((n_seq, 2, w), F32),
        ],
        scratch_shapes=[pltpu.VMEM((nslab, scratch_rows, LRU_BLK), F32) for _ in range(4)],
        compiler_params=_cparams(("arbitrary", "arbitrary")),
        name="rglru",
    )(z, z, conv_w, conv_b.reshape(1, w), wcat, bcat, lam, h0)


def _merge_kernel(ac_ref, al_ref, rc_ref, rl_ref, ga_ref, gr_ref, o_ref, *, n_ctx_tiles, eps, wa):
    i = pl.program_id(0)

    def emit(a_ref, r_ref):
        o_ref[:, :wa] = _rms(a_ref[...], ga_ref[...], eps).astype(o_ref.dtype)
        o_ref[:, wa:] = _rms(r_ref[...], gr_ref[...], eps).astype(o_ref.dtype)

    @pl.when(i < n_ctx_tiles)
    def _():
        emit(ac_ref, rc_ref)

    @pl.when(i >= n_ctx_tiles)
    def _():
        emit(al_ref, rl_ref)


def merge_norm(cfg, attn_c, attn_l, rec_c, rec_l, g_mla, g_lru):
    tr = cfg.row_tile
    wa = attn_c.shape[1]
    wr = rec_c.shape[1]
    nct = cfg.n_ctx // tr
    nlt = cfg.n_lat // tr
    ctx_map = lambda i: (jnp.minimum(i, nct - 1), 0)
    lat_map = lambda i: (jnp.maximum(i - nct, 0), 0)
    return pl.pallas_call(
        functools.partial(_merge_kernel, n_ctx_tiles=nct, eps=cfg.eps, wa=wa),
        grid=(nct + nlt,),
        in_specs=[
            pl.BlockSpec((tr, wa), ctx_map),
            pl.BlockSpec((tr, wa), lat_map),
            pl.BlockSpec((tr, wr), ctx_map),
            pl.BlockSpec((tr, wr), lat_map),
            pl.BlockSpec((1, wa), lambda i: (0, 0)),
            pl.BlockSpec((1, wr), lambda i: (0, 0)),
        ],
        out_specs=pl.BlockSpec((tr, wa + wr), lambda i: (i, 0)),
        out_shape=jax.ShapeDtypeStruct((cfg.n_all, wa + wr), BF16),
        compiler_params=_cparams(("arbitrary",)),
        name="merge_norm",
    )(attn_c, attn_l, rec_c, rec_l, g_mla.reshape(1, wa), g_lru.reshape(1, wr))


def _split_bf16(x):
    hi = x.astype(BF16)
    lo = (x - hi.astype(F32)).astype(BF16)
    return hi, lo


def _router_kernel(x_ref, g_ref, sh_ref, sc_ref, wh_ref, wl_ref, y_ref, idx_ref, gate_ref, *, eps, n_experts):
    y = _rms(x_ref[...], g_ref[...], eps)
    y = y * (1.0 + sc_ref[0]) + sh_ref[0]
    y_ref[...] = y
    yh, yl = _split_bf16(y)
    wh = wh_ref[...]
    logits = (jnp.dot(yh, wh, preferred_element_type=F32)
              + jnp.dot(yh, wl_ref[...], preferred_element_type=F32)
              + jnp.dot(yl, wh, preferred_element_type=F32))
    lane = lax.broadcasted_iota(jnp.int32, logits.shape, 1)
    neg = jnp.float32(-jnp.inf)
    l1 = jnp.where(lane < n_experts, logits, neg)
    m1 = jnp.max(l1, axis=-1, keepdims=True)
    i1 = jnp.min(jnp.where(l1 == m1, lane, LANES), axis=-1, keepdims=True)
    l2 = jnp.where(lane == i1, neg, l1)
    m2 = jnp.max(l2, axis=-1, keepdims=True)
    i2 = jnp.min(jnp.where(l2 == m2, lane, LANES), axis=-1, keepdims=True)
    e = jnp.exp(m2 - m1)
    den = 1.0 + e
    idx_ref[...] = jnp.where(lane == 0, i1, jnp.where(lane == 1, i2, 0))
    gate_ref[...] = jnp.where(lane == 0, 1.0 / den, jnp.where(lane == 1, e / den, 0.0))


def router_top2(cfg, x, g, shift, scale, wr_hi, wr_lo):
    n, d = x.shape
    tr = cfg.row_tile
    sel = _mod_sel(cfg, tr)
    return pl.pallas_call(
        functools.partial(_router_kernel, eps=cfg.eps, n_experts=cfg.n_experts),
        grid=(n // tr,),
        in_specs=[
            pl.BlockSpec((tr, d), lambda i: (i, 0)),
            pl.BlockSpec((1, d), lambda i: (0, 0)),
            pl.BlockSpec((1, 1, d), lambda i: (sel(i), 0, 0)),
            pl.BlockSpec((1, 1, d), lambda i: (sel(i), 0, 0)),
            pl.BlockSpec((d, LANES), lambda i: (0, 0)),
            pl.BlockSpec((d, LANES), lambda i: (0, 0)),
        ],
        out_specs=[
            pl.BlockSpec((tr, d), lambda i: (i, 0)),
            pl.BlockSpec((tr, LANES), lambda i: (i, 0)),
            pl.BlockSpec((tr, LANES), lambda i: (i, 0)),
        ],
        out_shape=[
            jax.ShapeDtypeStruct((n, d), F32),
            jax.ShapeDtypeStruct((n, LANES), jnp.int32),
            jax.ShapeDtypeStruct((n, LANES), F32),
        ],
        compiler_params=_cparams(("arbitrary",)),
        name="router_top2",
    )(x, g.reshape(1, d), shift, scale, wr_hi, wr_lo)


def _gather_kernel(cur_ref, nxt_ref, x_hbm, o_ref, buf_ref, sems, *, rows, nsteps):
    i = pl.program_id(0)
    slot = i % 2

    def row_copy(idx_ref, r, s):
        return pltpu.make_async_copy(x_hbm.at[pl.ds(idx_ref[0, 0, r], 1), :],
                                     buf_ref.at[s, pl.ds(r, 1), :], sems.at[s])

    def issue(idx_ref, s):
        def body(r, c):
            row_copy(idx_ref, r, s).start()
            return c
        lax.fori_loop(0, rows, body, 0, unroll=8)

    @pl.when(i == 0)
    def _():
        issue(cur_ref, slot)

    @pl.when(i + 1 < nsteps)
    def _():
        issue(nxt_ref, 1 - slot)

    def wait(r, c):
        row_copy(cur_ref, r, slot).wait()
        return c

    lax.fori_loop(0, rows, wait, 0, unroll=8)
    o_ref[...] = buf_ref[slot].astype(o_ref.dtype)


def gather_rows(cfg, x, idx_sorted):
    n, d = x.shape
    p = idx_sorted.shape[0]
    rows = cfg.gather_rows
    nsteps = p // rows
    idx = idx_sorted.reshape(nsteps, 1, rows)
    return pl.pallas_call(
        functools.partial(_gather_kernel, rows=rows, nsteps=nsteps),
        grid=(nsteps,),
        in_specs=[
            pl.BlockSpec((1, 1, rows), lambda i: (i, 0, 0), memory_space=pltpu.SMEM),
            pl.BlockSpec((1, 1, rows), lambda i: (jnp.minimum(i + 1, nsteps - 1), 0, 0), memory_space=pltpu.SMEM),
            pl.BlockSpec(memory_space=pl.ANY),
        ],
        out_specs=pl.BlockSpec((rows, d), lambda i: (i, 0)),
        out_shape=jax.ShapeDtypeStruct((p, d), BF16),
        scratch_shapes=[pltpu.VMEM((2, rows, d), F32), pltpu.SemaphoreType.DMA((2,))],
        compiler_params=_cparams(("arbitrary",)),
        name="moe_gather",
    )(idx, idx, x)


def _experts_kernel(te_ref, nu_ref, xs_ref, wg_ref, wu_ref, wd_ref, gs_ref, o_ref, h_ref, *, nf, tf):
    i = pl.program_id(0)
    s = pl.program_id(1)
    live = i < nu_ref[0]

    @pl.when(live & (s < nf))
    def _():
        x = xs_ref[...]
        g = jnp.dot(x, wg_ref[0], preferred_element_type=F32)
        u = jnp.dot(x, wu_ref[0], preferred_element_type=F32)
        h_ref[s] = (g * jax.nn.sigmoid(g) * u).astype(BF16)

    @pl.when(live & (s >= nf))
    def _():
        acc = jnp.dot(h_ref[0], wd_ref[0, 0:tf, :], preferred_element_type=F32)
        for f in range(1, nf):
            acc += jnp.dot(h_ref[f], wd_ref[0, f * tf:(f + 1) * tf, :], preferred_element_type=F32)
        o_ref[...] = gs_ref[...] * acc

    @pl.when(jnp.logical_not(live) & (s >= nf))
    def _():
        o_ref[...] = jnp.zeros_like(o_ref)


def grouped_experts(cfg, xs, tile_expert, n_used, wg, wu, wd, gate_sorted):
    p, d = xs.shape
    fdim = wg.shape[2]
    tm = cfg.moe_tm
    tf = _pick_tile(fdim, cfg.moe_tf)
    tn = _pick_tile(d, cfg.moe_tn)
    nf = fdim // tf
    nn = d // tn
    nt = p // tm

    def live(i, nu):
        return i < nu[0]

    def row_map(i, s, te, nu):
        return (jnp.minimum(i, nu[0] - 1), 0)

    def w_up_map(i, s, te, nu):
        return (te[i], 0, jnp.where(live(i, nu), jnp.minimum(s, nf - 1), nf - 1))

    def out_col(s):
        return jnp.clip(s - nf, 0, nn - 1)

    def w_dn_map(i, s, te, nu):
        return (te[i], 0, jnp.where(live(i, nu), out_col(s), nn - 1))

    grid_spec = pltpu.PrefetchScalarGridSpec(
        num_scalar_prefetch=2,
        grid=(nt, nf + nn),
        in_specs=[
            pl.BlockSpec((tm, d), row_map),
            pl.BlockSpec((1, d, tf), w_up_map),
            pl.BlockSpec((1, d, tf), w_up_map),
            pl.BlockSpec((1, fdim, tn), w_dn_map),
            pl.BlockSpec((tm, 1), row_map),
        ],
        out_specs=pl.BlockSpec((tm, tn), lambda i, s, te, nu: (i, out_col(s))),
        scratch_shapes=[pltpu.VMEM((nf, tm, tf), BF16)],
    )
    return pl.pallas_call(
        functools.partial(_experts_kernel, nf=nf, tf=tf),
        grid_spec=grid_spec,
        out_shape=jax.ShapeDtypeStruct((p, d), F32),
        compiler_params=_cparams(("arbitrary", "arbitrary")),
        name="moe_experts",
    )(tile_expert, n_used, xs, wg, wu, wd, gate_sorted.reshape(p, 1))


def _combine_kernel(p1_ref, p2_ref, n1_ref, n2_ref, ys_hbm, x_ref, g_ref, o_ref, buf_ref, sems, *, rows, nsteps):
    i = pl.program_id(0)
    slot = i % 2

    def row_copy(pos_ref, k, r, s):
        return pltpu.make_async_copy(ys_hbm.at[pl.ds(pos_ref[0, 0, r], 1), :],
                                     buf_ref.at[s, k, pl.ds(r, 1), :], sems.at[s])

    def issue(a_ref, b_ref, s):
        def body(r, c):
            row_copy(a_ref, 0, r, s).start()
            row_copy(b_ref, 1, r, s).start()
            return c
        lax.fori_loop(0, rows, body, 0, unroll=8)

    @pl.when(i == 0)
    def _():
        issue(p1_ref, p2_ref, slot)

    @pl.when(i + 1 < nsteps)
    def _():
        issue(n1_ref, n2_ref, 1 - slot)

    def wait(r, c):
        row_copy(p1_ref, 0, r, slot).wait()
        row_copy(p2_ref, 1, r, slot).wait()
        return c

    lax.fori_loop(0, rows, wait, 0, unroll=8)
    o_ref[...] = x_ref[...] + g_ref[0] * (buf_ref[slot, 0] + buf_ref[slot, 1])


def moe_combine(cfg, x, ys, pos1, pos2, gate):
    n, d = x.shape
    rows = cfg.combine_rows
    nsteps = n // rows
    sel = _mod_sel(cfg, rows)
    cur = lambda i: (i, 0, 0)
    nxt = lambda i: (jnp.minimum(i + 1, nsteps - 1), 0, 0)
    p1 = pos1.reshape(nsteps, 1, rows)
    p2 = pos2.reshape(nsteps, 1, rows)
    return pl.pallas_call(
        functools.partial(_combine_kernel, rows=rows, nsteps=nsteps),
        grid=(nsteps,),
        in_specs=[
            pl.BlockSpec((1, 1, rows), cur, memory_space=pltpu.SMEM),
            pl.BlockSpec((1, 1, rows), cur, memory_space=pltpu.SMEM),
            pl.BlockSpec((1, 1, rows), nxt, memory_space=pltpu.SMEM),
            pl.BlockSpec((1, 1, rows), nxt, memory_space=pltpu.SMEM),
            pl.BlockSpec(memory_space=pl.ANY),
            pl.BlockSpec((rows, d), lambda i: (i, 0)),
            pl.BlockSpec((1, 1, d), lambda i: (sel(i), 0, 0)),
        ],
        out_specs=pl.BlockSpec((rows, d), lambda i: (i, 0)),
        out_shape=jax.ShapeDtypeStruct((n, d), F32),
        scratch_shapes=[pltpu.VMEM((2, 2, rows, d), F32), pltpu.SemaphoreType.DMA((2,))],
        compiler_params=_cparams(("arbitrary",)),
        name="moe_combine",
    )(p1, p2, p1, p2, ys, x, gate)


def _routing_tables(cfg, idx, gates):
    n = idx.shape[0]
    ne = cfg.n_experts
    tm = cfg.moe_tm
    p = _round_up(2 * n, tm) + ne * tm
    nt = p // tm
    flat_e = idx[:, :2].reshape(-1)
    flat_g = gates[:, :2].reshape(-1)
    onehot = (flat_e[:, None] == jnp.arange(ne, dtype=jnp.int32)[None, :]).astype(jnp.int32)
    csum = jnp.cumsum(onehot, axis=0)
    rank = jnp.sum(csum * onehot, axis=1) - 1
    counts = csum[-1]
    tiles_e = (counts + tm - 1) // tm
    tiles_end = jnp.cumsum(tiles_e)
    row_off = (tiles_end - tiles_e) * tm
    pos = jnp.sum(onehot * row_off[None, :], axis=1) + rank
    n_used = tiles_end[-1]
    tile_ids = jnp.arange(nt, dtype=jnp.int32)
    tile_expert = jnp.sum((tile_ids[:, None] >= tiles_end[None, :]).astype(jnp.int32), axis=1)
    last_expert = jnp.sum((n_used - 1 >= tiles_end).astype(jnp.int32))
    tile_expert = jnp.where(tile_ids < n_used, tile_expert, last_expert).astype(jnp.int32)
    tok = jnp.arange(2 * n, dtype=jnp.int32) // 2
    idx_sorted = jnp.zeros((p,), jnp.int32).at[pos].set(tok)
    gate_sorted = jnp.zeros((p,), F32).at[pos].set(flat_g)
    pos = pos.reshape(n, 2).astype(jnp.int32)
    return idx_sorted, gate_sorted, pos[:, 0], pos[:, 1], tile_expert, n_used.reshape(1).astype(jnp.int32)


def moe_ffn(cfg, x, g, shift, scale, gate, wr_hi, wr_lo, wg, wu, wd):
    h2, idx, gates = router_top2(cfg, x, g, shift, scale, wr_hi, wr_lo)
    idx_sorted, gate_sorted, pos1, pos2, tile_expert, n_used = _routing_tables(cfg, idx, gates)
    xs = gather_rows(cfg, h2, idx_sorted)
    ys = grouped_experts(cfg, xs, tile_expert, n_used, wg, wu, wd, gate_sorted)
    return moe_combine(cfg, x, ys, pos1, pos2, gate)


def _rope_swap_perm():
    q = QK_ROPE // 4
    return jnp.array(list(range(q, 2 * q)) + list(range(0, q)) + list(range(3 * q, 4 * q)) + list(range(2 * q, 3 * q)))


def _rope_tables(cfg):
    t = jnp.arange(cfg.dec_seq)
    half = QK_ROPE // 2
    inv = cfg.rope_theta ** (-jnp.arange(half // 2, dtype=F32) * 2.0 / half)
    ang_r = (t // cfg.grid_w).astype(F32)[:, None] * inv[None, :]
    ang_c = (t % cfg.grid_w).astype(F32)[:, None] * inv[None, :]
    c = jnp.concatenate([jnp.cos(ang_r), jnp.cos(ang_r), jnp.cos(ang_c), jnp.cos(ang_c)], axis=1)
    s = jnp.concatenate([-jnp.sin(ang_r), jnp.sin(ang_r), -jnp.sin(ang_c), jnp.sin(ang_c)], axis=1)
    c = jnp.concatenate([jnp.ones((cfg.n_ctx, QK_ROPE), F32), jnp.tile(c, (cfg.dec_batch, 1))], axis=0)
    s = jnp.concatenate([jnp.zeros((cfg.n_ctx, QK_ROPE), F32), jnp.tile(s, (cfg.dec_batch, 1))], axis=0)
    return jnp.concatenate([c, s], axis=1), jnp.concatenate([c, c, s, s], axis=1)


def _prep_layer_weights(cfg, w_in, w_q_up, lru_w_a, lru_w_x, lru_b_a, lru_b_x):
    ql, kl, w = cfg.q_lora, cfg.kv_lora, cfg.lru_w
    perm = _rope_swap_perm()
    o_kv = ql
    o_kr = ql + kl
    o_xb = o_kr + QK_ROPE
    wq_c = w_in[:, :ql]
    wkv_c = w_in[:, o_kv:o_kr]
    wkr = w_in[:, o_kr:o_xb]
    wkr_sw = wkr[:, perm]
    w_all = jnp.concatenate([wkv_c, wkr, wkr, wkr_sw, wkr_sw, wq_c, w_in[:, o_xb:]], axis=1).astype(BF16)
    wq = w_q_up.reshape(ql, cfg.n_heads, QK_NOPE + QK_ROPE)
    wq_rope = wq[..., QK_NOPE:]
    wq = jnp.concatenate([wq[..., :QK_NOPE], wq_rope, wq_rope[..., perm]], axis=-1)
    wq = wq.reshape(ql, cfg.n_heads * HEAD_PITCH).astype(BF16)
    wcat = jnp.concatenate([lru_w_a[0], lru_w_x[0], lru_w_a[1], lru_w_x[1]], axis=-1).astype(BF16)
    nb = w // LRU_BLK
    bsplit = lambda b: b.reshape(2, nb, 1, LRU_BLK)
    ba, bx = bsplit(lru_b_a), bsplit(lru_b_x)
    bcat = jnp.concatenate([ba[0], bx[0], ba[1], bx[1]], axis=-1)
    return w_all, wq, wcat, bcat


def _pad_ff(w, axis, to):
    pad = [(0, 0)] * w.ndim
    pad[axis] = (0, to - w.shape[axis])
    return jnp.pad(w, pad)


def _forward(cfg, x_prompt, x_sample, c, cache_ckv, cache_k_rope, state_lru, c_ctx, w_ada, b_ada,
             norm_mix, norm_ffn, w_in, q_a_norm, w_q_up, kv_a_norm, w_kv_up, conv_w, conv_b,
             lru_w_a, lru_b_a, lru_w_x, lru_b_x, lru_lambda, out_norm_mla, out_norm_lru, w_out,
             ffn_w_gate, ffn_w_up, ffn_w_down, router_w, exp_w_gate, exp_w_up, exp_w_down,
             final_norm_g):
    d = cfg.d_model
    n_ctx, n_lat = cfg.n_ctx, cfg.n_lat
    n_mod = 1 + cfg.dec_batch
    assert n_mod <= SUBLANES

    x = jnp.concatenate([x_prompt.reshape(n_ctx, d), x_sample.reshape(n_lat, d)], axis=0)

    cond8 = jnp.zeros((SUBLANES, d), F32).at[0].set(c_ctx).at[1:n_mod].set(c)
    mods = ada_modulation(cond8, w_ada, b_ada)
    mods = mods.reshape(cfg.depth, SUBLANES, 6, 1, d)

    lru_col0 = cfg.kv_lora + 2 * LANES + cfg.q_lora
    q_col_block = (cfg.kv_lora + 2 * LANES) // cfg.q_lora
    assert (cfg.kv_lora + 2 * LANES) % cfg.q_lora == 0
    ff_pad = _round_up(cfg.d_ff, cfg.ff_align)

    q_tab, k_tab = _rope_tables(cfg)
    h0_ctx = jnp.zeros((cfg.batch, 2, cfg.lru_w), F32)
    krc_all = jnp.concatenate([cache_k_rope, cache_k_rope], axis=-1).astype(BF16)

    ckv_list, krope_list, lru_list = [], [], []
    for l in range(cfg.depth):
        shift_m, scale_m, gate_m, shift_f, scale_f, gate_f = (mods[l, :, k] for k in range(6))
        w_all, wq, wcat, bcat = _prep_layer_weights(cfg, w_in[l], w_q_up[l], lru_w_a[l], lru_w_x[l],
                                                    lru_b_a[l], lru_b_x[l])
        w_kv = w_kv_up[l].astype(BF16)

        h1 = adaln(cfg, x, norm_mix[l], shift_m, scale_m)
        z = matmul(cfg, h1, w_all, F32, tm=cfg.mm_tm, tn=cfg.mm_tn)
        q = q_up(cfg, z, q_a_norm[l], wq, col_block=q_col_block)
        kv, ckv, krot = kv_up(cfg, z, kv_a_norm[l], k_tab, w_kv)
        kvc = cache_kv_up(cfg, cache_ckv[:, l].reshape(cfg.dec_batch * cfg.past_len, cfg.kv_lora), w_kv)
        krc = krc_all[:, l].reshape(cfg.dec_batch * cfg.past_len, LANES)

        attn_c = attention(cfg, q, kv, krot, q_tab, row0=0, n_seq=cfg.batch, seq_len=cfg.seq,
                           tq=cfg.seq, hpb=cfg.n_heads)
        attn_l = attention(cfg, q, kv, krot, q_tab, row0=n_ctx, n_seq=cfg.dec_batch, seq_len=cfg.dec_seq,
                           tq=min(cfg.attn_tq, cfg.dec_seq), hpb=cfg.attn_hpb, cache=(kvc, krc))
        lam = lru_lambda[l]
        rec_c, st_c = rglru(cfg, z, conv_w[l], conv_b[l], wcat, bcat, lam, h0_ctx,
                            col0=lru_col0, row0=0, n_seq=cfg.batch, seq_len=cfg.seq)
        rec_l, _ = rglru(cfg, z, conv_w[l], conv_b[l], wcat, bcat, lam, state_lru[:, l],
                         col0=lru_col0, row0=n_ctx, n_seq=cfg.dec_batch, seq_len=cfg.dec_seq)
        merged = merge_norm(cfg, attn_c, attn_l, rec_c, rec_l, out_norm_mla[l], out_norm_lru[l])
        x = matmul(cfg, merged, w_out[l].astype(BF16), F32, tm=cfg.mm_tm, tn=cfg.mm_tn, resid=x, gate=gate_m)

        ckv_list.append(ckv[:n_ctx].reshape(cfg.batch, cfg.seq, cfg.kv_lora))
        krope_list.append(z[:n_ctx, cfg.kv_lora:cfg.kv_lora + QK_ROPE].reshape(cfg.batch, cfg.seq, QK_ROPE))
        lru_list.append(st_c)

        j = l // 2
        if l % 2 == 0:
            ffg = _pad_ff(ffn_w_gate[j], 1, ff_pad).astype(BF16)
            ffu = _pad_ff(ffn_w_up[j], 1, ff_pad).astype(BF16)
            ffd = _pad_ff(ffn_w_down[j], 0, ff_pad).astype(BF16)
            h2 = adaln(cfg, x, norm_ffn[l], shift_f, scale_f)
            hh = swiglu_up(cfg, h2, ffg, ffu)
            x = matmul(cfg, hh, ffd, F32, tm=cfg.down_tm, tn=cfg.mm_tn, resid=x, gate=gate_f)
        else:
            wr = _pad_ff(router_w[j], 1, LANES)
            wr_hi = wr.astype(BF16)
            wr_lo = (wr - wr_hi.astype(F32)).astype(BF16)
            x = moe_ffn(cfg, x, norm_ffn[l], shift_f, scale_f, gate_f, wr_hi, wr_lo,
                        exp_w_gate[j].astype(BF16), exp_w_up[j].astype(BF16), exp_w_down[j].astype(BF16))

    y_prompt = final_norm(cfg, x, final_norm_g, 0, n_ctx).reshape(cfg.batch, cfg.seq, d)
    y_sample = final_norm(cfg, x, final_norm_g, n_ctx, n_lat).reshape(cfg.dec_batch, cfg.dec_seq, d)
    new_ckv = jnp.stack(ckv_list, axis=1)
    new_k_rope = jnp.stack(krope_list, axis=1)
    new_lru_state = jnp.stack(lru_list, axis=1)
    return (y_prompt, y_sample, new_ckv, new_k_rope, new_lru_state)


def kernel(x_prompt, x_sample, c, cache_ckv, cache_k_rope, state_lru, c_ctx, w_ada, b_ada, norm_mix, norm_ffn, w_in, q_a_norm, w_q_up, kv_a_norm, w_kv_up, conv_w, conv_b, lru_w_a, lru_b_a, lru_w_x, lru_b_x, lru_lambda, out_norm_mla, out_norm_lru, w_out, ffn_w_gate, ffn_w_up, ffn_w_down, router_w, exp_w_gate, exp_w_up, exp_w_down, final_norm):
    return _forward(Cfg(), x_prompt, x_sample, c, cache_ckv, cache_k_rope, state_lru, c_ctx, w_ada, b_ada,
                    norm_mix, norm_ffn, w_in, q_a_norm, w_q_up, kv_a_norm, w_kv_up, conv_w, conv_b,
                    lru_w_a, lru_b_a, lru_w_x, lru_b_x, lru_lambda, out_norm_mla, out_norm_lru, w_out,
                    ffn_w_gate, ffn_w_up, ffn_w_down, router_w, exp_w_gate, exp_w_up, exp_w_down,
                    final_norm)
```

```python
import functools
from typing import NamedTuple

import jax
import jax.numpy as jnp
from jax import lax
from jax.experimental import pallas as pl
from jax.experimental.pallas import tpu as pltpu

F32 = jnp.float32
BF16 = jnp.bfloat16

LANES = 128
SUBLANES = 8
VMEM_LIMIT_BYTES = 56 * 1024 * 1024

QK_NOPE = 128
QK_ROPE = 64
V_DIM = 128
LRU_BLK = 128
HEAD_PITCH = 2 * LANES


class Cfg(NamedTuple):
    d_model: int = 4096
    batch: int = 32
    seq: int = 256
    depth: int = 4
    dec_batch: int = 2
    dec_seq: int = 2048
    past_len: int = 256
    grid_w: int = 64
    n_heads: int = 16
    q_lora: int = 768
    kv_lora: int = 512
    lru_w: int = 2048
    conv_w: int = 4
    lru_c: float = 8.0
    d_ff: int = 11008
    n_experts: int = 8
    d_expert: int = 2048
    eps: float = 1e-6
    rope_theta: float = 10000.0
    mm_tm: int = 1024
    mm_tn: int = 512
    down_tm: int = 512
    up_tm: int = 256
    up_tn: int = 4096
    ff_align: int = 1024
    row_tile: int = 256
    attn_tq: int = 256
    attn_hpb: int = 2
    moe_tm: int = 1024
    moe_tf: int = 512
    moe_tn: int = 512
    gather_rows: int = 512
    combine_rows: int = 256
    lru_cw: int = 256
    lru_chunk: int = 256

    @property
    def n_ctx(self):
        return self.batch * self.seq

    @property
    def n_lat(self):
        return self.dec_batch * self.dec_seq

    @property
    def n_all(self):
        return self.n_ctx + self.n_lat


def _cparams(sem):
    return pltpu.CompilerParams(dimension_semantics=sem, vmem_limit_bytes=VMEM_LIMIT_BYTES)


def _pick_tile(n, pref):
    t = pref
    while n % t:
        t //= 2
    assert t >= LANES or t == n, (n, pref)
    return t


def _round_up(n, m):
    return (n + m - 1) // m * m


def _mod_sel(cfg, rows_per_tile):
    def sel(i):
        row0 = i * rows_per_tile
        return jnp.where(row0 < cfg.n_ctx, 0, 1 + (row0 - cfg.n_ctx) // cfg.dec_seq)
    return sel


def _ada_kernel(c_ref, w_ref, b_ref, o_ref):
    c = c_ref[...]
    a = (c * jax.nn.sigmoid(c)).astype(BF16)
    o_ref[0] = jnp.dot(a, w_ref[0].astype(BF16), preferred_element_type=F32) + b_ref[0]


def ada_modulation(cond8, w_ada, b_ada):
    depth, d, n = w_ada.shape
    tn = _pick_tile(n, 512)
    return pl.pallas_call(
        _ada_kernel,
        grid=(depth, n // tn),
        in_specs=[
            pl.BlockSpec((SUBLANES, d), lambda l, j: (0, 0)),
            pl.BlockSpec((1, d, tn), lambda l, j: (l, 0, j)),
            pl.BlockSpec((1, 1, tn), lambda l, j: (l, 0, j)),
        ],
        out_specs=pl.BlockSpec((1, SUBLANES, tn), lambda l, j: (l, 0, j)),
        out_shape=jax.ShapeDtypeStruct((depth, SUBLANES, n), F32),
        compiler_params=_cparams(("arbitrary", "arbitrary")),
        name="ada_modulation",
    )(cond8, w_ada, b_ada.reshape(depth, 1, n))


def _rms(x, g, eps):
    return x * lax.rsqrt(jnp.mean(x * x, axis=-1, keepdims=True) + eps) * g


def _adaln_kernel(x_ref, g_ref, sh_ref, sc_ref, o_ref, *, eps):
    y = _rms(x_ref[...], g_ref[...], eps)
    o_ref[...] = (y * (1.0 + sc_ref[0]) + sh_ref[0]).astype(o_ref.dtype)


def adaln(cfg, x, g, shift, scale):
    n, d = x.shape
    tr = cfg.row_tile
    sel = _mod_sel(cfg, tr)
    return pl.pallas_call(
        functools.partial(_adaln_kernel, eps=cfg.eps),
        grid=(n // tr,),
        in_specs=[
            pl.BlockSpec((tr, d), lambda i: (i, 0)),
            pl.BlockSpec((1, d), lambda i: (0, 0)),
            pl.BlockSpec((1, 1, d), lambda i: (sel(i), 0, 0)),
            pl.BlockSpec((1, 1, d), lambda i: (sel(i), 0, 0)),
        ],
        out_specs=pl.BlockSpec((tr, d), lambda i: (i, 0)),
        out_shape=jax.ShapeDtypeStruct((n, d), BF16),
        compiler_params=_cparams(("arbitrary",)),
        name="adaln",
    )(x, g.reshape(1, d), shift, scale)


def _final_norm_kernel(x_ref, g_ref, o_ref, *, eps):
    o_ref[...] = _rms(x_ref[...], g_ref[...], eps)


def final_norm(cfg, x, g, row0, nrows):
    d = x.shape[1]
    tr = cfg.row_tile
    b0 = row0 // tr
    return pl.pallas_call(
        functools.partial(_final_norm_kernel, eps=cfg.eps),
        grid=(nrows // tr,),
        in_specs=[
            pl.BlockSpec((tr, d), lambda i: (b0 + i, 0)),
            pl.BlockSpec((1, d), lambda i: (0, 0)),
        ],
        out_specs=pl.BlockSpec((tr, d), lambda i: (i, 0)),
        out_shape=jax.ShapeDtypeStruct((nrows, d), F32),
        compiler_params=_cparams(("arbitrary",)),
        name="final_norm",
    )(x, g.reshape(1, d))


def _mm_kernel(*refs, gated):
    if gated:
        x_ref, w_ref, r_ref, g_ref, o_ref = refs
    else:
        x_ref, w_ref, o_ref = refs
    acc = jnp.dot(x_ref[...], w_ref[...], preferred_element_type=F32)
    if gated:
        o_ref[...] = r_ref[...] + g_ref[0] * acc
    else:
        o_ref[...] = acc.astype(o_ref.dtype)


def matmul(cfg, x, w, out_dtype, *, tm, tn, resid=None, gate=None):
    m, kd = x.shape
    n = w.shape[1]
    tm = _pick_tile(m, tm)
    tn = _pick_tile(n, tn)
    gated = resid is not None
    in_specs = [
        pl.BlockSpec((tm, kd), lambda i, j: (i, 0)),
        pl.BlockSpec((kd, tn), lambda i, j: (0, j)),
    ]
    args = [x, w]
    if gated:
        sel = _mod_sel(cfg, tm)
        in_specs += [
            pl.BlockSpec((tm, tn), lambda i, j: (i, j)),
            pl.BlockSpec((1, 1, tn), lambda i, j: (sel(i), 0, j)),
        ]
        args += [resid, gate]
    return pl.pallas_call(
        functools.partial(_mm_kernel, gated=gated),
        grid=(m // tm, n // tn),
        in_specs=in_specs,
        out_specs=pl.BlockSpec((tm, tn), lambda i, j: (i, j)),
        out_shape=jax.ShapeDtypeStruct((m, n), out_dtype),
        compiler_params=_cparams(("arbitrary", "arbitrary")),
        name="matmul_gated" if gated else "matmul",
    )(*args)


def _swiglu_up_kernel(x_ref, wg_ref, wu_ref, o_ref):
    x = x_ref[...]
    g = jnp.dot(x, wg_ref[...], preferred_element_type=F32)
    u = jnp.dot(x, wu_ref[...], preferred_element_type=F32)
    o_ref[...] = (g * jax.nn.sigmoid(g) * u).astype(o_ref.dtype)


def swiglu_up(cfg, x, wg, wu):
    m, kd = x.shape
    n = wg.shape[1]
    tm = _pick_tile(m, cfg.mm_tm)
    tn = _pick_tile(n, cfg.mm_tn)
    return pl.pallas_call(
        _swiglu_up_kernel,
        grid=(m // tm, n // tn),
        in_specs=[
            pl.BlockSpec((tm, kd), lambda i, j: (i, 0)),
            pl.BlockSpec((kd, tn), lambda i, j: (0, j)),
            pl.BlockSpec((kd, tn), lambda i, j: (0, j)),
        ],
        out_specs=pl.BlockSpec((tm, tn), lambda i, j: (i, j)),
        out_shape=jax.ShapeDtypeStruct((m, n), BF16),
        compiler_params=_cparams(("arbitrary", "arbitrary")),
        name="swiglu_up",
    )(x, wg, wu)


def _q_up_kernel(x_ref, g_ref, w_ref, o_ref, *, eps):
    y = _rms(x_ref[...], g_ref[...], eps).astype(BF16)
    o_ref[...] = jnp.dot(y, w_ref[...], preferred_element_type=F32).astype(o_ref.dtype)


def q_up(cfg, z, g, w, col_block):
    n = z.shape[0]
    kd, nout = w.shape
    tm = _pick_tile(n, cfg.up_tm)
    tn = _pick_tile(nout, cfg.up_tn)
    return pl.pallas_call(
        functools.partial(_q_up_kernel, eps=cfg.eps),
        grid=(n // tm, nout // tn),
        in_specs=[
            pl.BlockSpec((tm, kd), lambda i, j: (i, col_block)),
            pl.BlockSpec((1, kd), lambda i, j: (0, 0)),
            pl.BlockSpec((kd, tn), lambda i, j: (0, j)),
        ],
        out_specs=pl.BlockSpec((tm, tn), lambda i, j: (i, j)),
        out_shape=jax.ShapeDtypeStruct((n, nout), BF16),
        compiler_params=_cparams(("arbitrary", "arbitrary")),
        name="q_up",
    )(z, g.reshape(1, kd), w)


def _kv_up_kernel(x_ref, g_ref, kr_ref, tab_ref, w_ref, kv_ref, ckv_ref, krot_ref, *, eps):
    j = pl.program_id(1)
    ckv = _rms(x_ref[...], g_ref[...], eps)
    kv_ref[...] = jnp.dot(ckv.astype(BF16), w_ref[...], preferred_element_type=F32).astype(kv_ref.dtype)

    @pl.when(j == 0)
    def _():
        ckv_ref[...] = ckv
        kr = kr_ref[...]
        tab = tab_ref[...]
        krot = kr[:, :LANES] * tab[:, :LANES] + kr[:, LANES:] * tab[:, LANES:]
        krot_ref[...] = krot.astype(krot_ref.dtype)


def kv_up(cfg, z, g, k_tab, w):
    n = z.shape[0]
    kd, nout = w.shape
    assert kd % (2 * LANES) == 0
    tm = _pick_tile(n, cfg.up_tm)
    tn = _pick_tile(nout, cfg.up_tn)
    kr_block = kd // (2 * LANES)
    return pl.pallas_call(
        functools.partial(_kv_up_kernel, eps=cfg.eps),
        grid=(n // tm, nout // tn),
        in_specs=[
            pl.BlockSpec((tm, kd), lambda i, j: (i, 0)),
            pl.BlockSpec((1, kd), lambda i, j: (0, 0)),
            pl.BlockSpec((tm, 2 * LANES), lambda i, j: (i, kr_block)),
            pl.BlockSpec((tm, 2 * LANES), lambda i, j: (i, 0)),
            pl.BlockSpec((kd, tn), lambda i, j: (0, j)),
        ],
        out_specs=[
            pl.BlockSpec((tm, tn), lambda i, j: (i, j)),
            pl.BlockSpec((tm, kd), lambda i, j: (i, 0)),
            pl.BlockSpec((tm, LANES), lambda i, j: (i, 0)),
        ],
        out_shape=[
            jax.ShapeDtypeStruct((n, nout), BF16),
            jax.ShapeDtypeStruct((n, kd), F32),
            jax.ShapeDtypeStruct((n, LANES), BF16),
        ],
        compiler_params=_cparams(("arbitrary", "arbitrary")),
        name="kv_up",
    )(z, g.reshape(1, kd), z, k_tab, w)


def _cache_kv_kernel(x_ref, w_ref, o_ref):
    o_ref[...] = jnp.dot(x_ref[...].astype(BF16), w_ref[...], preferred_element_type=F32).astype(o_ref.dtype)


def cache_kv_up(cfg, ckv, w):
    n, kd = ckv.shape
    nout = w.shape[1]
    tm = _pick_tile(n, cfg.up_tm)
    tn = _pick_tile(nout, cfg.up_tn)
    return pl.pallas_call(
        _cache_kv_kernel,
        grid=(n // tm, nout // tn),
        in_specs=[
            pl.BlockSpec((tm, kd), lambda i, j: (i, 0)),
            pl.BlockSpec((kd, tn), lambda i, j: (0, j)),
        ],
        out_specs=pl.BlockSpec((tm, tn), lambda i, j: (i, j)),
        out_shape=jax.ShapeDtypeStruct((n, nout), BF16),
        compiler_params=_cparams(("arbitrary", "arbitrary")),
        name="cache_kv_up",
    )(ckv, w)


def _attn_kernel(*refs, hpb, scale, has_cache):
    if has_cache:
        cs_ref, q_ref, kv_ref, kr_ref, kvc_ref, krc_ref, o_ref = refs
    else:
        cs_ref, q_ref, kv_ref, kr_ref, o_ref = refs
    cs = cs_ref[...]
    kr = kr_ref[...]
    nt = (((1,), (1,)), ((), ()))
    for h in range(hpb):
        c0 = h * HEAD_PITCH
        qn = q_ref[:, c0:c0 + LANES]
        qr = (q_ref[:, c0 + LANES:c0 + HEAD_PITCH].astype(F32) * cs).astype(BF16)
        qc = jnp.concatenate([qn, qr], axis=1)
        kc = jnp.concatenate([kv_ref[:, c0:c0 + LANES], kr], axis=1)
        v = kv_ref[:, c0 + LANES:c0 + HEAD_PITCH]
        s = lax.dot_general(qc, kc, nt, preferred_element_type=F32) * scale
        m = jnp.max(s, axis=-1, keepdims=True)
        if has_cache:
            kcc = jnp.concatenate([kvc_ref[:, c0:c0 + LANES], krc_ref[...]], axis=1)
            vc = kvc_ref[:, c0 + LANES:c0 + HEAD_PITCH]
            sc = lax.dot_general(qc, kcc, nt, preferred_element_type=F32) * scale
            m = jnp.maximum(m, jnp.max(sc, axis=-1, keepdims=True))
        p = jnp.exp(s - m)
        l = jnp.sum(p, axis=-1, keepdims=True)
        o = jnp.dot(p.astype(BF16), v, preferred_element_type=F32)
        if has_cache:
            pc = jnp.exp(sc - m)
            l = l + jnp.sum(pc, axis=-1, keepdims=True)
            o = o + jnp.dot(pc.astype(BF16), vc, preferred_element_type=F32)
        o_ref[:, h * V_DIM:(h + 1) * V_DIM] = o / l


def attention(cfg, q, kv, kr, q_tab, *, row0, n_seq, seq_len, tq, hpb, cache=None):
    nh = cfg.n_heads
    assert seq_len % tq == 0 and row0 % seq_len == 0 and nh % hpb == 0
    nq = seq_len // tq
    qb0 = row0 // tq
    sb0 = row0 // seq_len
    scale = float((QK_NOPE + QK_ROPE) ** -0.5)
    in_specs = [
        pl.BlockSpec((tq, LANES), lambda b, h, i: (qb0 + b * nq + i, 0)),
        pl.BlockSpec((tq, hpb * HEAD_PITCH), lambda b, h, i: (qb0 + b * nq + i, h)),
        pl.BlockSpec((seq_len, hpb * HEAD_PITCH), lambda b, h, i: (sb0 + b, h)),
        pl.BlockSpec((seq_len, LANES), lambda b, h, i: (sb0 + b, 0)),
    ]
    args = [q_tab, q, kv, kr]
    if cache is not None:
        kvc, krc = cache
        past = kvc.shape[0] // n_seq
        in_specs += [
            pl.BlockSpec((past, hpb * HEAD_PITCH), lambda b, h, i: (b, h)),
            pl.BlockSpec((past, LANES), lambda b, h, i: (b, 0)),
        ]
        args += [kvc, krc]
    return pl.pallas_call(
        functools.partial(_attn_kernel, hpb=hpb, scale=scale, has_cache=cache is not None),
        grid=(n_seq, nh // hpb, nq),
        in_specs=in_specs,
        out_specs=pl.BlockSpec((tq, hpb * V_DIM), lambda b, h, i: (b * nq + i, h)),
        out_shape=jax.ShapeDtypeStruct((n_seq * seq_len, nh * V_DIM), F32),
        compiler_params=_cparams(("arbitrary", "arbitrary", "arbitrary")),
        name="attention_cached" if cache is not None else "attention",
    )(*args)


def _softplus(x):
    return jnp.maximum(x, 0.0) + jnp.log1p(jnp.exp(-jnp.abs(x)))


def _sigmoid(x):
    return 0.5 * jnp.tanh(0.5 * x) + 0.5


def _sqrt_nonneg(z):
    return jnp.where(z > 0.0, z * lax.rsqrt(z), 0.0)


def _lru_kernel(xb_ref, gb_ref, cw_ref, cb_ref, w_ref, b_ref, lam_ref, h0_ref,
                rec_ref, st_ref, af_ref, uf_ref, ab_ref, ub_ref, *, seq_len, chunk, nslab, lru_c):
    seg = seq_len // SUBLANES
    pitch = seg + SUBLANES
    piece = min(chunk, seg)
    npiece = chunk // piece
    cps = seg // piece
    nchunk = seq_len // chunk
    cw = cw_ref[...]
    cb = cb_ref[...]
    sp = _softplus(-lam_ref[...])
    dirs = ((af_ref, uf_ref), (ab_ref, ub_ref))

    def scratch_row(c, k):
        g = c * npiece + k
        return pl.multiple_of((g // cps) * pitch + (g % cps) * piece, SUBLANES)

    def gate_chunk(c, carry):
        r0 = pl.multiple_of(c * chunk, chunk)
        cur = xb_ref[pl.ds(r0, chunk), :]
        prev_start = pl.multiple_of(jnp.maximum(r0 - SUBLANES, 0), SUBLANES)
        next_start = pl.multiple_of(jnp.minimum(r0 + chunk, seq_len - SUBLANES), SUBLANES)
        prev = jnp.where(c > 0, xb_ref[pl.ds(prev_start, SUBLANES), :], 0.0)
        nxt = jnp.where(c < nchunk - 1, xb_ref[pl.ds(next_start, SUBLANES), :], 0.0)
        ext = jnp.concatenate([prev, cur, nxt], axis=0)
        rows = chunk + 2 * SUBLANES
        xm2 = pltpu.roll(ext, 2, axis=0)[SUBLANES:SUBLANES + chunk]
        xm1 = pltpu.roll(ext, 1, axis=0)[SUBLANES:SUBLANES + chunk]
        xp1 = pltpu.roll(ext, rows - 1, axis=0)[SUBLANES:SUBLANES + chunk]
        xc = cw[0:1] * xm2 + cw[1:2] * xm1 + cw[2:3] * cur + cw[3:4] * xp1 + cb
        for n in range(nslab):
            lo = n * LRU_BLK
            xn = xc[:, lo:lo + LRU_BLK]
            g = jnp.dot(xn.astype(BF16), w_ref[n], preferred_element_type=F32) + b_ref[n]
            for d, (a_ref, u_ref) in enumerate(dirs):
                r = _sigmoid(g[:, (2 * d) * LRU_BLK:(2 * d + 1) * LRU_BLK])
                i = _sigmoid(g[:, (2 * d + 1) * LRU_BLK:(2 * d + 2) * LRU_BLK])
                log_a = (-lru_c) * r * sp[d:d + 1, lo:lo + LRU_BLK]
                t = jnp.tanh(log_a)
                a = jnp.exp(log_a)
                u = _sqrt_nonneg(-2.0 * t / (1.0 - t)) * (i * xn)
                for k in range(npiece):
                    q0 = scratch_row(c, k)
                    a_ref[n, pl.ds(q0, piece), :] = a[k * piece:(k + 1) * piece]
                    u_ref[n, pl.ds(q0, piece), :] = u[k * piece:(k + 1) * piece]
        return carry

    lax.fori_loop(0, nchunk, gate_chunk, 0)

    def local_time(d, s):
        return s if d == 0 else seg - 1 - s

    def strided(ref, n, s):
        return ref.at[n, pl.ds(s, SUBLANES, stride=pitch), :]

    def pass1(s, carry):
        out = []
        for d, (a_ref, u_ref) in enumerate(dirs):
            for n in range(nslab):
                e, p = carry[d * nslab + n]
                a = strided(a_ref, n, local_time(d, s))[...]
                u = strided(u_ref, n, local_time(d, s))[...]
                out.append((a * e + u, a * p))
        return tuple(out)

    zero = jnp.zeros((SUBLANES, LRU_BLK), F32)
    one = jnp.ones((SUBLANES, LRU_BLK), F32)
    ends = lax.fori_loop(0, seg, pass1, tuple((zero, one) for _ in range(2 * nslab)), unroll=4)

    h0 = h0_ref[0]
    entries = []
    finals = [[], []]
    for d in range(2):
        order = range(SUBLANES) if d == 0 else range(SUBLANES - 1, -1, -1)
        for n in range(nslab):
            e, p = ends[d * nslab + n]
            h = h0[d:d + 1, n * LRU_BLK:(n + 1) * LRU_BLK]
            rows = [None] * SUBLANES
            for j in order:
                rows[j] = h
                h = p[j:j + 1] * h + e[j:j + 1]
            entries.append(jnp.concatenate(rows, axis=0))
            finals[d].append(h)
    st_ref[0] = jnp.concatenate([jnp.concatenate(finals[0], axis=1), jnp.concatenate(finals[1], axis=1)], axis=0)

    def pass2(s, carry):
        out = []
        for d, (a_ref, u_ref) in enumerate(dirs):
            for n in range(nslab):
                h = carry[d * nslab + n]
                a = strided(a_ref, n, local_time(d, s))[...]
                u_view = strided(u_ref, n, local_time(d, s))
                h = a * h + u_view[...]
                u_view[...] = h
                out.append(h)
        return tuple(out)

    lax.fori_loop(0, seg, pass2, tuple(entries), unroll=4)

    def out_chunk(c, carry):
        r0 = pl.multiple_of(c * chunk, chunk)
        slabs = []
        for n in range(nslab):
            pieces = []
            for k in range(npiece):
                q0 = scratch_row(c, k)
                pieces.append(uf_ref[n, pl.ds(q0, piece), :] + ub_ref[n, pl.ds(q0, piece), :])
            slabs.append(jnp.concatenate(pieces, axis=0))
        h = jnp.concatenate(slabs, axis=1)
        rec_ref[pl.ds(r0, chunk), :] = h * jax.nn.gelu(gb_ref[pl.ds(r0, chunk), :])
        return carry

    lax.fori_loop(0, nchunk, out_chunk, 0)


def rglru(cfg, z, conv_w, conv_b, wcat, bcat, lam, h0, *, col0, row0, n_seq, seq_len):
    w = cfg.lru_w
    cwid = min(cfg.lru_cw, w)
    nslab = cwid // LRU_BLK
    ncb = w // cwid
    seg = seq_len // SUBLANES
    chunk = min(cfg.lru_chunk, seq_len)
    sb0 = row0 // seq_len
    cb0 = col0 // cwid
    assert row0 % seq_len == 0 and col0 % cwid == 0 and seq_len % chunk == 0 and seg % SUBLANES == 0
    assert chunk % seg == 0 or seg % chunk == 0
    scratch_rows = SUBLANES * (seg + SUBLANES)
    return pl.pallas_call(
        functools.partial(_lru_kernel, seq_len=seq_len, chunk=chunk, nslab=nslab, lru_c=cfg.lru_c),
        grid=(n_seq, ncb),
        in_specs=[
            pl.BlockSpec((seq_len, cwid), lambda b, c: (sb0 + b, cb0 + c)),
            pl.BlockSpec((seq_len, cwid), lambda b, c: (sb0 + b, cb0 + ncb + c)),
            pl.BlockSpec((cfg.conv_w, cwid), lambda b, c: (0, c)),
            pl.BlockSpec((1, cwid), lambda b, c: (0, c)),
            pl.BlockSpec((nslab, LRU_BLK, 4 * LRU_BLK), lambda b, c: (c, 0, 0)),
            pl.BlockSpec((nslab, 1, 4 * LRU_BLK), lambda b, c: (c, 0, 0)),
            pl.BlockSpec((2, cwid), lambda b, c: (0, c)),
            pl.BlockSpec((1, 2, cwid), lambda b, c: (b, 0, c)),
        ],
        out_specs=[
            pl.BlockSpec((seq_len, cwid), lambda b, c: (b, c)),
            pl.BlockSpec((1, 2, cwid), lambda b, c: (b, 0, c)),
        ],
        out_shape=[
            jax.ShapeDtypeStruct((n_seq * seq_len, w), F32),
            jax.ShapeDtypeStruct((n_seq, 2, w), F32),
        ],
        scratch_shapes=[pltpu.VMEM((nslab, scratch_rows, LRU_BLK), F32) for _ in range(4)],
        compiler_params=_cparams(("arbitrary", "arbitrary")),
        name="rglru",
    )(z, z, conv_w, conv_b.reshape(1, w), wcat, bcat, lam, h0)


def _merge_kernel(ac_ref, al_ref, rc_ref, rl_ref, ga_ref, gr_ref, o_ref, *, n_ctx_tiles, eps, wa):
    i = pl.program_id(0)

    def emit(a_ref, r_ref):
        o_ref[:, :wa] = _rms(a_ref[...], ga_ref[...], eps).astype(o_ref.dtype)
        o_ref[:, wa:] = _rms(r_ref[...], gr_ref[...], eps).astype(o_ref.dtype)

    @pl.when(i < n_ctx_tiles)
    def _():
        emit(ac_ref, rc_ref)

    @pl.when(i >= n_ctx_tiles)
    def _():
        emit(al_ref, rl_ref)


def merge_norm(cfg, attn_c, attn_l, rec_c, rec_l, g_mla, g_lru):
    tr = cfg.row_tile
    wa = attn_c.shape[1]
    wr = rec_c.shape[1]
    nct = cfg.n_ctx // tr
    nlt = cfg.n_lat // tr
    ctx_map = lambda i: (jnp.minimum(i, nct - 1), 0)
    lat_map = lambda i: (jnp.maximum(i - nct, 0), 0)
    return pl.pallas_call(
        functools.partial(_merge_kernel, n_ctx_tiles=nct, eps=cfg.eps, wa=wa),
        grid=(nct + nlt,),
        in_specs=[
            pl.BlockSpec((tr, wa), ctx_map),
            pl.BlockSpec((tr, wa), lat_map),
            pl.BlockSpec((tr, wr), ctx_map),
            pl.BlockSpec((tr, wr), lat_map),
            pl.BlockSpec((1, wa), lambda i: (0, 0)),
            pl.BlockSpec((1, wr), lambda i: (0, 0)),
        ],
        out_specs=pl.BlockSpec((tr, wa + wr), lambda i: (i, 0)),
        out_shape=jax.ShapeDtypeStruct((cfg.n_all, wa + wr), BF16),
        compiler_params=_cparams(("arbitrary",)),
        name="merge_norm",
    )(attn_c, attn_l, rec_c, rec_l, g_mla.reshape(1, wa), g_lru.reshape(1, wr))


def _split_bf16(x):
    hi = x.astype(BF16)
    lo = (x - hi.astype(F32)).astype(BF16)
    return hi, lo


def _router_kernel(x_ref, g_ref, sh_ref, sc_ref, wh_ref, wl_ref, y_ref, idx_ref, gate_ref, *, eps, n_experts):
    y = _rms(x_ref[...], g_ref[...], eps)
    y = y * (1.0 + sc_ref[0]) + sh_ref[0]
    yh, yl = _split_bf16(y)
    half = y.shape[1] // 2
    bits = pltpu.bitcast(yh.astype(F32), jnp.uint32)
    y_ref[...] = (bits[:, half:] & jnp.uint32(0xFFFF0000)) | lax.shift_right_logical(bits[:, :half], jnp.uint32(16))
    wh = wh_ref[...]
    logits = (jnp.dot(yh, wh, preferred_element_type=F32)
              + jnp.dot(yh, wl_ref[...], preferred_element_type=F32)
              + jnp.dot(yl, wh, preferred_element_type=F32))
    lane = lax.broadcasted_iota(jnp.int32, logits.shape, 1)
    neg = jnp.float32(-jnp.inf)
    l1 = jnp.where(lane < n_experts, logits, neg)
    m1 = jnp.max(l1, axis=-1, keepdims=True)
    i1 = jnp.min(jnp.where(l1 == m1, lane, LANES), axis=-1, keepdims=True)
    l2 = jnp.where(lane == i1, neg, l1)
    m2 = jnp.max(l2, axis=-1, keepdims=True)
    i2 = jnp.min(jnp.where(l2 == m2, lane, LANES), axis=-1, keepdims=True)
    e = jnp.exp(m2 - m1)
    den = 1.0 + e
    idx_ref[...] = jnp.where(lane == 0, i1, jnp.where(lane == 1, i2, 0))
    gate_ref[...] = jnp.where(lane == 0, 1.0 / den, jnp.where(lane == 1, e / den, 0.0))


def router_top2(cfg, x, g, shift, scale, wr_hi, wr_lo):
    n, d = x.shape
    tr = cfg.row_tile
    sel = _mod_sel(cfg, tr)
    return pl.pallas_call(
        functools.partial(_router_kernel, eps=cfg.eps, n_experts=cfg.n_experts),
        grid=(n // tr,),
        in_specs=[
            pl.BlockSpec((tr, d), lambda i: (i, 0)),
            pl.BlockSpec((1, d), lambda i: (0, 0)),
            pl.BlockSpec((1, 1, d), lambda i: (sel(i), 0, 0)),
            pl.BlockSpec((1, 1, d), lambda i: (sel(i), 0, 0)),
            pl.BlockSpec((d, LANES), lambda i: (0, 0)),
            pl.BlockSpec((d, LANES), lambda i: (0, 0)),
        ],
        out_specs=[
            pl.BlockSpec((tr, d // 2), lambda i: (i, 0)),
            pl.BlockSpec((tr, LANES), lambda i: (i, 0)),
            pl.BlockSpec((tr, LANES), lambda i: (i, 0)),
        ],
        out_shape=[
            jax.ShapeDtypeStruct((n, d // 2), jnp.uint32),
            jax.ShapeDtypeStruct((n, LANES), jnp.int32),
            jax.ShapeDtypeStruct((n, LANES), F32),
        ],
        compiler_params=_cparams(("arbitrary",)),
        name="router_top2",
    )(x, g.reshape(1, d), shift, scale, wr_hi, wr_lo)


def _gather_kernel(cur_ref, nxt_ref, x_hbm, o_ref, buf_ref, sems, *, rows, nsteps):
    i = pl.program_id(0)
    slot = i % 2

    def row_copy(idx_ref, r, s):
        return pltpu.make_async_copy(x_hbm.at[pl.ds(idx_ref[0, 0, r], 1), :],
                                     buf_ref.at[s, pl.ds(r, 1), :], sems.at[s])

    def issue(idx_ref, s):
        def body(r, c):
            row_copy(idx_ref, r, s).start()
            return c
        lax.fori_loop(0, rows, body, 0, unroll=8)

    @pl.when(i == 0)
    def _():
        issue(cur_ref, slot)

    @pl.when(i + 1 < nsteps)
    def _():
        issue(nxt_ref, 1 - slot)

    def wait(r, c):
        row_copy(cur_ref, r, slot).wait()
        return c

    lax.fori_loop(0, rows, wait, 0, unroll=8)
    packed = buf_ref[slot]
    half = packed.shape[1]
    o_ref[:, :half] = pltpu.bitcast(lax.shift_left(packed, jnp.uint32(16)), F32).astype(o_ref.dtype)
    o_ref[:, half:] = pltpu.bitcast(packed & jnp.uint32(0xFFFF0000), F32).astype(o_ref.dtype)


def gather_rows(cfg, x, idx_sorted):
    n, hd = x.shape
    d = 2 * hd
    p = idx_sorted.shape[0]
    rows = cfg.gather_rows
    nsteps = p // rows
    idx = idx_sorted.reshape(nsteps, 1, rows)
    return pl.pallas_call(
        functools.partial(_gather_kernel, rows=rows, nsteps=nsteps),
        grid=(nsteps,),
        in_specs=[
            pl.BlockSpec((1, 1, rows), lambda i: (i, 0, 0), memory_space=pltpu.SMEM),
            pl.BlockSpec((1, 1, rows), lambda i: (jnp.minimum(i + 1, nsteps - 1), 0, 0), memory_space=pltpu.SMEM),
            pl.BlockSpec(memory_space=pl.ANY),
        ],
        out_specs=pl.BlockSpec((rows, d), lambda i: (i, 0)),
        out_shape=jax.ShapeDtypeStruct((p, d), BF16),
        scratch_shapes=[pltpu.VMEM((2, rows, hd), jnp.uint32), pltpu.SemaphoreType.DMA((2,))],
        compiler_params=_cparams(("arbitrary",)),
        name="moe_gather",
    )(idx, idx, x)


def _experts_kernel(te_ref, nu_ref, xs_ref, wg_ref, wu_ref, wd_ref, gs_ref, o_ref, h_ref, *, nf, tf):
    i = pl.program_id(0)
    s = pl.program_id(1)
    live = i < nu_ref[0]

    @pl.when(live & (s < nf))
    def _():
        x = xs_ref[...]
        g = jnp.dot(x, wg_ref[0], preferred_element_type=F32)
        u = jnp.dot(x, wu_ref[0], preferred_element_type=F32)
        h_ref[s] = (g * jax.nn.sigmoid(g) * u).astype(BF16)

    @pl.when(live & (s >= nf))
    def _():
        acc = jnp.dot(h_ref[0], wd_ref[0, 0:tf, :], preferred_element_type=F32)
        for f in range(1, nf):
            acc += jnp.dot(h_ref[f], wd_ref[0, f * tf:(f + 1) * tf, :], preferred_element_type=F32)
        o_ref[...] = gs_ref[...] * acc

    @pl.when(jnp.logical_not(live) & (s >= nf))
    def _():
        o_ref[...] = jnp.zeros_like(o_ref)


def grouped_experts(cfg, xs, tile_expert, n_used, wg, wu, wd, gate_sorted):
    p, d = xs.shape
    fdim = wg.shape[2]
    tm = cfg.moe_tm
    tf = _pick_tile(fdim, cfg.moe_tf)
    tn = _pick_tile(d, cfg.moe_tn)
    nf = fdim // tf
    nn = d // tn
    nt = p // tm

    def live(i, nu):
        return i < nu[0]

    def row_map(i, s, te, nu):
        return (jnp.minimum(i, nu[0] - 1), 0)

    def w_up_map(i, s, te, nu):
        return (te[i], 0, jnp.where(live(i, nu), jnp.minimum(s, nf - 1), nf - 1))

    def out_col(s):
        return jnp.clip(s - nf, 0, nn - 1)

    def w_dn_map(i, s, te, nu):
        return (te[i], 0, jnp.where(live(i, nu), out_col(s), nn - 1))

    grid_spec = pltpu.PrefetchScalarGridSpec(
        num_scalar_prefetch=2,
        grid=(nt, nf + nn),
        in_specs=[
            pl.BlockSpec((tm, d), row_map),
            pl.BlockSpec((1, d, tf), w_up_map),
            pl.BlockSpec((1, d, tf), w_up_map),
            pl.BlockSpec((1, fdim, tn), w_dn_map),
            pl.BlockSpec((tm, 1), row_map),
        ],
        out_specs=pl.BlockSpec((tm, tn), lambda i, s, te, nu: (i, out_col(s))),
        scratch_shapes=[pltpu.VMEM((nf, tm, tf), BF16)],
    )
    return pl.pallas_call(
        functools.partial(_experts_kernel, nf=nf, tf=tf),
        grid_spec=grid_spec,
        out_shape=jax.ShapeDtypeStruct((p, d), F32),
        compiler_params=_cparams(("arbitrary", "arbitrary")),
        name="moe_experts",
    )(tile_expert, n_used, xs, wg, wu, wd, gate_sorted.reshape(p, 1))


def _combine_kernel(p1_ref, p2_ref, n1_ref, n2_ref, ys_hbm, x_ref, g_ref, o_ref, buf_ref, sems, *, rows, nsteps):
    i = pl.program_id(0)
    slot = i % 2

    def row_copy(pos_ref, k, r, s):
        return pltpu.make_async_copy(ys_hbm.at[pl.ds(pos_ref[0, 0, r], 1), :],
                                     buf_ref.at[s, k, pl.ds(r, 1), :], sems.at[s])

    def issue(a_ref, b_ref, s):
        def body(r, c):
            row_copy(a_ref, 0, r, s).start()
            row_copy(b_ref, 1, r, s).start()
            return c
        lax.fori_loop(0, rows, body, 0, unroll=8)

    @pl.when(i == 0)
    def _():
        issue(p1_ref, p2_ref, slot)

    @pl.when(i + 1 < nsteps)
    def _():
        issue(n1_ref, n2_ref, 1 - slot)

    def wait(r, c):
        row_copy(p1_ref, 0, r, slot).wait()
        row_copy(p2_ref, 1, r, slot).wait()
        return c

    lax.fori_loop(0, rows, wait, 0, unroll=8)
    o_ref[...] = x_ref[...] + g_ref[0] * (buf_ref[slot, 0] + buf_ref[slot, 1])


def moe_combine(cfg, x, ys, pos1, pos2, gate):
    n, d = x.shape
    rows = cfg.combine_rows
    nsteps = n // rows
    sel = _mod_sel(cfg, rows)
    cur = lambda i: (i, 0, 0)
    nxt = lambda i: (jnp.minimum(i + 1, nsteps - 1), 0, 0)
    p1 = pos1.reshape(nsteps, 1, rows)
    p2 = pos2.reshape(nsteps, 1, rows)
    return pl.pallas_call(
        functools.partial(_combine_kernel, rows=rows, nsteps=nsteps),
        grid=(nsteps,),
        in_specs=[
            pl.BlockSpec((1, 1, rows), cur, memory_space=pltpu.SMEM),
            pl.BlockSpec((1, 1, rows), cur, memory_space=pltpu.SMEM),
            pl.BlockSpec((1, 1, rows), nxt, memory_space=pltpu.SMEM),
            pl.BlockSpec((1, 1, rows), nxt, memory_space=pltpu.SMEM),
            pl.BlockSpec(memory_space=pl.ANY),
            pl.BlockSpec((rows, d), lambda i: (i, 0)),
            pl.BlockSpec((1, 1, d), lambda i: (sel(i), 0, 0)),
        ],
        out_specs=pl.BlockSpec((rows, d), lambda i: (i, 0)),
        out_shape=jax.ShapeDtypeStruct((n, d), F32),
        scratch_shapes=[pltpu.VMEM((2, 2, rows, d), F32), pltpu.SemaphoreType.DMA((2,))],
        compiler_params=_cparams(("arbitrary",)),
        name="moe_combine",
    )(p1, p2, p1, p2, ys, x, gate)


def _routing_tables(cfg, idx, gates):
    n = idx.shape[0]
    ne = cfg.n_experts
    tm = cfg.moe_tm
    p = _round_up(2 * n, tm) + ne * tm
    nt = p // tm
    flat_e = idx[:, :2].reshape(-1)
    flat_g = gates[:, :2].reshape(-1)
    onehot = (flat_e[:, None] == jnp.arange(ne, dtype=jnp.int32)[None, :]).astype(jnp.int32)
    csum = jnp.cumsum(onehot, axis=0)
    rank = jnp.sum(csum * onehot, axis=1) - 1
    counts = csum[-1]
    tiles_e = (counts + tm - 1) // tm
    tiles_end = jnp.cumsum(tiles_e)
    row_off = (tiles_end - tiles_e) * tm
    pos = jnp.sum(onehot * row_off[None, :], axis=1) + rank
    n_used = tiles_end[-1]
    tile_ids = jnp.arange(nt, dtype=jnp.int32)
    tile_expert = jnp.sum((tile_ids[:, None] >= tiles_end[None, :]).astype(jnp.int32), axis=1)
    last_expert = jnp.sum((n_used - 1 >= tiles_end).astype(jnp.int32))
    tile_expert = jnp.where(tile_ids < n_used, tile_expert, last_expert).astype(jnp.int32)
    tok = jnp.arange(2 * n, dtype=jnp.int32) // 2
    idx_sorted = jnp.zeros((p,), jnp.int32).at[pos].set(tok)
    gate_sorted = jnp.zeros((p,), F32).at[pos].set(flat_g)
    pos = pos.reshape(n, 2).astype(jnp.int32)
    return idx_sorted, gate_sorted, pos[:, 0], pos[:, 1], tile_expert, n_used.reshape(1).astype(jnp.int32)


def moe_ffn(cfg, x, g, shift, scale, gate, wr_hi, wr_lo, wg, wu, wd):
    h2, idx, gates = router_top2(cfg, x, g, shift, scale, wr_hi, wr_lo)
    idx_sorted, gate_sorted, pos1, pos2, tile_expert, n_used = _routing_tables(cfg, idx, gates)
    xs = gather_rows(cfg, h2, idx_sorted)
    ys = grouped_experts(cfg, xs, tile_expert, n_used, wg, wu, wd, gate_sorted)
    return moe_combine(cfg, x, ys, pos1, pos2, gate)


def _rope_swap_perm():
    q = QK_ROPE // 4
    return jnp.array(list(range(q, 2 * q)) + list(range(0, q)) + list(range(3 * q, 4 * q)) + list(range(2 * q, 3 * q)))


def _rope_tables(cfg):
    t = jnp.arange(cfg.dec_seq)
    half = QK_ROPE // 2
    inv = cfg.rope_theta ** (-jnp.arange(half // 2, dtype=F32) * 2.0 / half)
    ang_r = (t // cfg.grid_w).astype(F32)[:, None] * inv[None, :]
    ang_c = (t % cfg.grid_w).astype(F32)[:, None] * inv[None, :]
    c = jnp.concatenate([jnp.cos(ang_r), jnp.cos(ang_r), jnp.cos(ang_c), jnp.cos(ang_c)], axis=1)
    s = jnp.concatenate([-jnp.sin(ang_r), jnp.sin(ang_r), -jnp.sin(ang_c), jnp.sin(ang_c)], axis=1)
    c = jnp.concatenate([jnp.ones((cfg.n_ctx, QK_ROPE), F32), jnp.tile(c, (cfg.dec_batch, 1))], axis=0)
    s = jnp.concatenate([jnp.zeros((cfg.n_ctx, QK_ROPE), F32), jnp.tile(s, (cfg.dec_batch, 1))], axis=0)
    return jnp.concatenate([c, s], axis=1), jnp.concatenate([c, c, s, s], axis=1)


def _prep_layer_weights(cfg, w_in, w_q_up, lru_w_a, lru_w_x, lru_b_a, lru_b_x):
    ql, kl, w = cfg.q_lora, cfg.kv_lora, cfg.lru_w
    perm = _rope_swap_perm()
    o_kv = ql
    o_kr = ql + kl
    o_xb = o_kr + QK_ROPE
    wq_c = w_in[:, :ql]
    wkv_c = w_in[:, o_kv:o_kr]
    wkr = w_in[:, o_kr:o_xb]
    wkr_sw = wkr[:, perm]
    w_all = jnp.concatenate([wkv_c, wkr, wkr, wkr_sw, wkr_sw, wq_c, w_in[:, o_xb:]], axis=1).astype(BF16)
    wq = w_q_up.reshape(ql, cfg.n_heads, QK_NOPE + QK_ROPE)
    wq_rope = wq[..., QK_NOPE:]
    wq = jnp.concatenate([wq[..., :QK_NOPE], wq_rope, wq_rope[..., perm]], axis=-1)
    wq = wq.reshape(ql, cfg.n_heads * HEAD_PITCH).astype(BF16)
    wcat = jnp.concatenate([lru_w_a[0], lru_w_x[0], lru_w_a[1], lru_w_x[1]], axis=-1).astype(BF16)
    nb = w // LRU_BLK
    bsplit = lambda b: b.reshape(2, nb, 1, LRU_BLK)
    ba, bx = bsplit(lru_b_a), bsplit(lru_b_x)
    bcat = jnp.concatenate([ba[0], bx[0], ba[1], bx[1]], axis=-1)
    return w_all, wq, wcat, bcat


def _pad_ff(w, axis, to):
    pad = [(0, 0)] * w.ndim
    pad[axis] = (0, to - w.shape[axis])
    return jnp.pad(w, pad)


def _forward(cfg, x_prompt, x_sample, c, cache_ckv, cache_k_rope, state_lru, c_ctx, w_ada, b_ada,
             norm_mix, norm_ffn, w_in, q_a_norm, w_q_up, kv_a_norm, w_kv_up, conv_w, conv_b,
             lru_w_a, lru_b_a, lru_w_x, lru_b_x, lru_lambda, out_norm_mla, out_norm_lru, w_out,
             ffn_w_gate, ffn_w_up, ffn_w_down, router_w, exp_w_gate, exp_w_up, exp_w_down,
             final_norm_g):
    d = cfg.d_model
    n_ctx, n_lat = cfg.n_ctx, cfg.n_lat
    n_mod = 1 + cfg.dec_batch
    assert n_mod <= SUBLANES

    x = jnp.concatenate([x_prompt.reshape(n_ctx, d), x_sample.reshape(n_lat, d)], axis=0)

    cond8 = jnp.zeros((SUBLANES, d), F32).at[0].set(c_ctx).at[1:n_mod].set(c)
    mods = ada_modulation(cond8, w_ada, b_ada)
    mods = mods.reshape(cfg.depth, SUBLANES, 6, 1, d)

    lru_col0 = cfg.kv_lora + 2 * LANES + cfg.q_lora
    q_col_block = (cfg.kv_lora + 2 * LANES) // cfg.q_lora
    assert (cfg.kv_lora + 2 * LANES) % cfg.q_lora == 0
    ff_pad = _round_up(cfg.d_ff, cfg.ff_align)

    q_tab, k_tab = _rope_tables(cfg)
    h0_ctx = jnp.zeros((cfg.batch, 2, cfg.lru_w), F32)
    krc_all = jnp.concatenate([cache_k_rope, cache_k_rope], axis=-1).astype(BF16)

    ckv_list, krope_list, lru_list = [], [], []
    for l in range(cfg.depth):
        shift_m, scale_m, gate_m, shift_f, scale_f, gate_f = (mods[l, :, k] for k in range(6))
        w_all, wq, wcat, bcat = _prep_layer_weights(cfg, w_in[l], w_q_up[l], lru_w_a[l], lru_w_x[l],
                                                    lru_b_a[l], lru_b_x[l])
        w_kv = w_kv_up[l].astype(BF16)

        h1 = adaln(cfg, x, norm_mix[l], shift_m, scale_m)
        z = matmul(cfg, h1, w_all, F32, tm=cfg.mm_tm, tn=cfg.mm_tn)
        q = q_up(cfg, z, q_a_norm[l], wq, col_block=q_col_block)
        kv, ckv, krot = kv_up(cfg, z, kv_a_norm[l], k_tab, w_kv)
        kvc = cache_kv_up(cfg, cache_ckv[:, l].reshape(cfg.dec_batch * cfg.past_len, cfg.kv_lora), w_kv)
        krc = krc_all[:, l].reshape(cfg.dec_batch * cfg.past_len, LANES)

        attn_c = attention(cfg, q, kv, krot, q_tab, row0=0, n_seq=cfg.batch, seq_len=cfg.seq,
                           tq=cfg.seq, hpb=cfg.n_heads)
        attn_l = attention(cfg, q, kv, krot, q_tab, row0=n_ctx, n_seq=cfg.dec_batch, seq_len=cfg.dec_seq,
                           tq=min(cfg.attn_tq, cfg.dec_seq), hpb=cfg.attn_hpb, cache=(kvc, krc))
        lam = lru_lambda[l]
        rec_c, st_c = rglru(cfg, z, conv_w[l], conv_b[l], wcat, bcat, lam, h0_ctx,
                            col0=lru_col0, row0=0, n_seq=cfg.batch, seq_len=cfg.seq)
        rec_l, _ = rglru(cfg, z, conv_w[l], conv_b[l], wcat, bcat, lam, state_lru[:, l],
                         col0=lru_col0, row0=n_ctx, n_seq=cfg.dec_batch, seq_len=cfg.dec_seq)
        merged = merge_norm(cfg, attn_c, attn_l, rec_c, rec_l, out_norm_mla[l], out_norm_lru[l])
        x = matmul(cfg, merged, w_out[l].astype(BF16), F32, tm=cfg.mm_tm, tn=cfg.mm_tn, resid=x, gate=gate_m)

        ckv_list.append(ckv[:n_ctx].reshape(cfg.batch, cfg.seq, cfg.kv_lora))
        krope_list.append(z[:n_ctx, cfg.kv_lora:cfg.kv_lora + QK_ROPE].reshape(cfg.batch, cfg.seq, QK_ROPE))
        lru_list.append(st_c)

        j = l // 2
        if l % 2 == 0:
            ffg = _pad_ff(ffn_w_gate[j], 1, ff_pad).astype(BF16)
            ffu = _pad_ff(ffn_w_up[j], 1, ff_pad).astype(BF16)
            ffd = _pad_ff(ffn_w_down[j], 0, ff_pad).astype(BF16)
            h2 = adaln(cfg, x, norm_ffn[l], shift_f, scale_f)
            hh = swiglu_up(cfg, h2, ffg, ffu)
            x = matmul(cfg, hh, ffd, F32, tm=cfg.down_tm, tn=cfg.mm_tn, resid=x, gate=gate_f)
        else:
            wr = _pad_ff(router_w[j], 1, LANES)
            wr_hi = wr.astype(BF16)
            wr_lo = (wr - wr_hi.astype(F32)).astype(BF16)
            x = moe_ffn(cfg, x, norm_ffn[l], shift_f, scale_f, gate_f, wr_hi, wr_lo,
                        exp_w_gate[j].astype(BF16), exp_w_up[j].astype(BF16), exp_w_down[j].astype(BF16))

    y_prompt = final_norm(cfg, x, final_norm_g, 0, n_ctx).reshape(cfg.batch, cfg.seq, d)
    y_sample = final_norm(cfg, x, final_norm_g, n_ctx, n_lat).reshape(cfg.dec_batch, cfg.dec_seq, d)
    new_ckv = jnp.stack(ckv_list, axis=1)
    new_k_rope = jnp.stack(krope_list, axis=1)
    new_lru_state = jnp.stack(lru_list, axis=1)
    return (y_prompt, y_sample, new_ckv, new_k_rope, new_lru_state)


def kernel(x_prompt, x_sample, c, cache_ckv, cache_k_rope, state_lru, c_ctx, w_ada, b_ada, norm_mix, norm_ffn, w_in, q_a_norm, w_q_up, kv_a_norm, w_kv_up, conv_w, conv_b, lru_w_a, lru_b_a, lru_w_x, lru_b_x, lru_lambda, out_norm_mla, out_norm_lru, w_out, ffn_w_gate, ffn_w_up, ffn_w_down, router_w, exp_w_gate, exp_w_up, exp_w_down, final_norm):
    return _forward(Cfg(), x_prompt, x_sample, c, cache_ckv, cache_k_rope, state_lru, c_ctx, w_ada, b_ada,
                    norm_mix, norm_ffn, w_in, q_a_norm, w_q_up, kv_a_norm, w_kv_up, conv_w, conv_b,
                    lru_w_a, lru_b_a, lru_w_x, lru_b_x, lru_lambda, out_norm_mla, out_norm_lru, w_out,
                    ffn_w_gate, ffn_w_up, ffn_w_down, router_w, exp_w_gate, exp_w_up, exp_w_down,
                    final_norm)
```
